```python
import jax, jax.numpy as jnp
from jax import lax
import numpy as np

D_MODEL = 2048
BATCH = 2
SEQ = 4096
DEPTH = 2
DEC_BATCH = 32
DEC_SEQ = 8
PAST_LEN = 8192
PAGE_SIZE = 128

HEAD_DIM = 128
N_HEADS = D_MODEL // HEAD_DIM
CONV_W = 3
D_FF = 256 * ((8 * D_MODEL // 3 + 255) // 256)
N_A_LAYERS = DEPTH // 2
N_B_LAYERS = DEPTH - N_A_LAYERS
Q_BLOCK = 128
LN_EPS = 1e-5
DN_ALPHA = (2.0 * DEPTH) ** 0.25
DN_BETA = (8.0 * DEPTH) ** -0.25
CACHE_KEY_MEAN_SCALE = 6.0

kernel_name = "yoco_shortconv_stickbreaking_macaron"


def layer_norm(x, g, b):
    xf = x.astype(jnp.float32)
    mu = jnp.mean(xf, axis=-1, keepdims=True)
    var = jnp.mean(jnp.square(xf - mu), axis=-1, keepdims=True)
    return ((xf - mu) * lax.rsqrt(var + LN_EPS) * g.astype(jnp.float32) + b.astype(jnp.float32)).astype(x.dtype)


def swiglu(x, w_in, w_out):
    a, g = jnp.split(x @ w_in, 2, axis=-1)
    return (jax.nn.silu(a) * g) @ w_out


def short_conv_mixer(x, prev, w_in, w_conv, w_out):
    S = x.shape[1]
    b_gate, c_gate, h = jnp.split(x @ w_in, 3, axis=-1)
    u = c_gate * h
    up = jnp.concatenate([prev.astype(u.dtype), u], axis=1)
    conv = w_conv[0] * up[:, 0:S]
    for i in range(1, CONV_W):
        conv = conv + w_conv[i] * up[:, i:i + S]
    y = (b_gate * conv) @ w_out
    return y, up[:, -(CONV_W - 1):]


def stick_breaking_attention(q, k, v, pos0):
    B, Sq, H, Dh = q.shape
    Sk = k.shape[1]
    blk = min(Q_BLOCK, Sq)
    n_blk = -(-Sq // blk)
    pad = n_blk * blk - Sq
    qp = jnp.pad(q, ((0, 0), (0, pad), (0, 0), (0, 0)))
    qb = qp.reshape(B, n_blk, blk, H, Dh).transpose(1, 0, 2, 3, 4)
    qpos = (pos0 + jnp.arange(n_blk * blk)).reshape(n_blk, blk)
    kpos = jnp.arange(Sk)
    kf = k.astype(jnp.float32)
    vf = v.astype(jnp.float32)
    scale = Dh ** -0.5

    def one_block(args):
        qi, pi = args
        z = jnp.einsum('bqhd,bkhd->bhqk', qi.astype(jnp.float32), kf) * scale
        causal = kpos[None, :] < pi[:, None]
        log_keep = jnp.where(causal, jax.nn.log_sigmoid(-z), 0.0)
        after = lax.cumsum(log_keep, axis=3, reverse=True) - log_keep
        w = jnp.where(causal, jnp.exp(jax.nn.log_sigmoid(z) + after), 0.0)
        return jnp.einsum('bhqk,bkhd->bqhd', w, vf)

    out = lax.map(one_block, (qb, qpos))
    out = out.transpose(1, 0, 2, 3, 4).reshape(B, n_blk * blk, H, Dh)[:, :Sq]
    return out.astype(q.dtype)


def trunk(x, conv_state, past_k, past_v, pos0, ln_g, ln_b, w_ffn_in, w_ffn_out,
          w_a_in, w_a_conv, w_a_out, w_kv, w_q, w_o):
    Bsz, S, _ = x.shape
    new_conv = []
    k_new = v_new = k_all = v_all = None
    for i in range(DEPTH):
        x = layer_norm(DN_ALPHA * x + 0.5 * swiglu(x, w_ffn_in[i, 0], w_ffn_out[i, 0]), ln_g[i, 0], ln_b[i, 0])
        if i < N_A_LAYERS:
            mix, st = short_conv_mixer(x, conv_state[i], w_a_in[i], w_a_conv[i], w_a_out[i])
            new_conv.append(st)
        else:
            j = i - N_A_LAYERS
            q = (x @ w_q[j]).reshape(Bsz, S, N_HEADS, HEAD_DIM)
            o = stick_breaking_attention(q, k_all, v_all, pos0)
            mix = o.reshape(Bsz, S, D_MODEL) @ w_o[j]
        x = layer_norm(DN_ALPHA * x + mix, ln_g[i, 1], ln_b[i, 1])
        x = layer_norm(DN_ALPHA * x + 0.5 * swiglu(x, w_ffn_in[i, 1], w_ffn_out[i, 1]), ln_g[i, 2], ln_b[i, 2])
        if i == N_A_LAYERS - 1:
            kv = x @ w_kv
            k_new, v_new = jnp.split(kv, 2, axis=-1)
            k_new = k_new.reshape(Bsz, S, N_HEADS, HEAD_DIM)
            v_new = v_new.reshape(Bsz, S, N_HEADS, HEAD_DIM)
            k_all = jnp.concatenate([past_k.astype(k_new.dtype), k_new], axis=1)
            v_all = jnp.concatenate([past_v.astype(v_new.dtype), v_new], axis=1)
    return x, jnp.stack(new_conv, axis=0), k_new, v_new


def setup_inputs(seed: int = 0) -> dict:
    key = jax.random.key(seed)
    ks = jax.random.split(key, 18)
    f32 = jnp.float32
    n_pages = PAST_LEN // PAGE_SIZE
    n_used = DEC_BATCH * n_pages
    n_phys = n_used + n_used // 4
    x_prompt = jax.random.normal(ks[0], (BATCH, SEQ, D_MODEL), f32)
    x_sample = jax.random.normal(ks[1], (DEC_BATCH, DEC_SEQ, D_MODEL), f32)
    state_conv = jax.random.normal(ks[2], (N_A_LAYERS, DEC_BATCH, CONV_W - 1, D_MODEL), f32)
    key_dir = jax.random.normal(ks[16], (N_HEADS, HEAD_DIM), f32)
    key_dir = key_dir * (CACHE_KEY_MEAN_SCALE * HEAD_DIM ** 0.5) / jnp.linalg.norm(key_dir, axis=-1, keepdims=True)
    cache_k = jax.random.normal(ks[3], (n_phys, PAGE_SIZE, N_HEADS, HEAD_DIM), f32) + key_dir
    cache_v = jax.random.normal(ks[4], (n_phys, PAGE_SIZE, N_HEADS, HEAD_DIM), f32) * DN_BETA
    page_table = jax.random.permutation(ks[5], n_phys)[:n_used].reshape(DEC_BATCH, n_pages).astype(jnp.int32)
    ln_g = 1.0 + 0.01 * jax.random.normal(ks[6], (DEPTH, 3, D_MODEL), f32)
    ln_b = 0.01 * jax.random.normal(ks[7], (DEPTH, 3, D_MODEL), f32)
    w_ffn_in = jax.random.normal(ks[8], (DEPTH, 2, D_MODEL, 2 * D_FF), f32) * D_MODEL ** -0.5
    w_ffn_out = jax.random.normal(ks[9], (DEPTH, 2, D_FF, D_MODEL), f32) * (D_FF ** -0.5 * DN_BETA)
    w_a_in = jax.random.normal(ks[10], (N_A_LAYERS, D_MODEL, 3 * D_MODEL), f32) * D_MODEL ** -0.5
    w_a_conv = jax.random.normal(ks[11], (N_A_LAYERS, CONV_W, D_MODEL), f32) * CONV_W ** -0.5
    w_a_out = jax.random.normal(ks[12], (N_A_LAYERS, D_MODEL, D_MODEL), f32) * (D_MODEL ** -0.5 * DN_BETA)
    w_kv = jax.random.normal(ks[13], (D_MODEL, 2 * D_MODEL), f32) * D_MODEL ** -0.5
    w_kv = w_kv.at[:, D_MODEL:].multiply(DN_BETA)
    w_q = jax.random.normal(ks[14], (N_B_LAYERS, D_MODEL, D_MODEL), f32) * D_MODEL ** -0.5
    w_o = jax.random.normal(ks[15], (N_B_LAYERS, D_MODEL, D_MODEL), f32) * (D_MODEL ** -0.5 * DN_BETA)
    return {"x_prompt": x_prompt, "x_sample": x_sample, "state_conv": state_conv,
            "cache_k": cache_k, "cache_v": cache_v, "page_table": page_table,
            "ln_g": ln_g, "ln_b": ln_b, "w_ffn_in": w_ffn_in, "w_ffn_out": w_ffn_out,
            "w_a_in": w_a_in, "w_a_conv": w_a_conv, "w_a_out": w_a_out,
            "w_kv": w_kv, "w_q": w_q, "w_o": w_o}


def reference(x_prompt, x_sample, state_conv, cache_k, cache_v, page_table, ln_g, ln_b,
              w_ffn_in, w_ffn_out, w_a_in, w_a_conv, w_a_out, w_kv, w_q, w_o):
    b_p = x_prompt.shape[0]
    empty = jnp.zeros((b_p, 0, N_HEADS, HEAD_DIM), x_prompt.dtype)
    conv0 = jnp.zeros((N_A_LAYERS, b_p, CONV_W - 1, D_MODEL), x_prompt.dtype)
    y_prompt, conv_prompt, k_prompt, v_prompt = trunk(
        x_prompt, conv0, empty, empty, 0, ln_g, ln_b, w_ffn_in, w_ffn_out,
        w_a_in, w_a_conv, w_a_out, w_kv, w_q, w_o)
    n_seq = page_table.shape[0]
    past_k = cache_k[page_table].reshape(n_seq, -1, N_HEADS, HEAD_DIM)
    past_v = cache_v[page_table].reshape(n_seq, -1, N_HEADS, HEAD_DIM)
    past_len = past_k.shape[1]
    y_sample, conv_sample, k_sample, v_sample = trunk(
        x_sample, state_conv, past_k, past_v, past_len, ln_g, ln_b, w_ffn_in, w_ffn_out,
        w_a_in, w_a_conv, w_a_out, w_kv, w_q, w_o)
    return (y_prompt, y_sample, conv_prompt, conv_sample, k_prompt, v_prompt, k_sample, v_sample)
```

```python
import functools

import jax
import jax.numpy as jnp
from jax import lax
from jax.experimental import pallas as pl
from jax.experimental.pallas import tpu as pltpu

F32 = jnp.float32
BF16 = jnp.bfloat16

LN_EPS = 1e-5
CONV_W = 3
SUBLANES = 8
VMEM_LIMIT_BYTES = 56 * 1024 * 1024


def _params():
    return pltpu.CompilerParams(vmem_limit_bytes=VMEM_LIMIT_BYTES)


def _pick_tile(total, candidates):
    for c in candidates:
        if total % c == 0:
            return c
    return total


def _swiglu_in_kernel(x_ref, wa_ref, wg_ref, o_ref):
    x = x_ref[...]
    a = jnp.dot(x, wa_ref[...], preferred_element_type=F32)
    g = jnp.dot(x, wg_ref[...], preferred_element_type=F32)
    o_ref[...] = (a * jax.nn.sigmoid(a) * g).astype(o_ref.dtype)


def swiglu_in(x16, w_in16):
    m_rows, d = x16.shape
    f = w_in16.shape[1] // 2
    tm = _pick_tile(m_rows, (1056, 1024, 512, 256, 128))
    tn = _pick_tile(f, (512, 256, 128))
    n_blocks = f // tn
    return pl.pallas_call(
        _swiglu_in_kernel,
        grid=(n_blocks, m_rows // tm),
        in_specs=[
            pl.BlockSpec((tm, d), lambda n, m: (m, 0)),
            pl.BlockSpec((d, tn), lambda n, m: (0, n)),
            pl.BlockSpec((d, tn), lambda n, m: (0, n + n_blocks)),
        ],
        out_specs=pl.BlockSpec((tm, tn), lambda n, m: (m, n)),
        out_shape=jax.ShapeDtypeStruct((m_rows, f), BF16),
        compiler_params=_params(),
        name="swiglu_in",
    )(x16, w_in16, w_in16)


def _proj_res_ln_kernel(a_ref, w_ref, res_ref, gain_ref, bias_ref, o32_ref, o16_ref, acc_ref,
                        *, alpha, coef, n_k):
    k = pl.program_id(1)
    part = jnp.dot(a_ref[...], w_ref[...], preferred_element_type=F32)

    if n_k > 1:
        @pl.when(k == 0)
        def _():
            acc_ref[...] = part

        @pl.when(jnp.logical_and(k > 0, k < n_k - 1))
        def _():
            acc_ref[...] += part

    @pl.when(k == n_k - 1)
    def _():
        total = part if n_k == 1 else acc_ref[...] + part
        y = alpha * res_ref[...] + coef * total
        mu = jnp.mean(y, axis=-1, keepdims=True)
        dev = y - mu
        var = jnp.mean(dev * dev, axis=-1, keepdims=True)
        out = dev * lax.rsqrt(var + LN_EPS) * gain_ref[...] + bias_ref[...]
        o32_ref[...] = out
        o16_ref[...] = out.astype(BF16)


def proj_res_ln(a16, w16, res32, gain, bias, *, alpha, coef):
    m_rows, k_dim = a16.shape
    d = w16.shape[1]
    tm = _pick_tile(m_rows, (528, 512, 256, 128))
    tk = k_dim if k_dim <= 2048 else _pick_tile(k_dim, (1408, 1024, 512, 256, 128))
    n_k = k_dim // tk
    kern = functools.partial(_proj_res_ln_kernel, alpha=alpha, coef=coef, n_k=n_k)
    return pl.pallas_call(
        kern,
        grid=(m_rows // tm, n_k),
        in_specs=[
            pl.BlockSpec((tm, tk), lambda m, k: (m, k)),
            pl.BlockSpec((tk, d), lambda m, k: (k, 0)),
            pl.BlockSpec((tm, d), lambda m, k: (m, 0)),
            pl.BlockSpec((1, d), lambda m, k: (0, 0)),
            pl.BlockSpec((1, d), lambda m, k: (0, 0)),
        ],
        out_specs=[
            pl.BlockSpec((tm, d), lambda m, k: (m, 0)),
            pl.BlockSpec((tm, d), lambda m, k: (m, 0)),
        ],
        out_shape=[jax.ShapeDtypeStruct((m_rows, d), F32), jax.ShapeDtypeStruct((m_rows, d), BF16)],
        scratch_shapes=[pltpu.VMEM((tm, d), F32)],
        compiler_params=_params(),
        name="proj_res_ln",
    )(a16, w16, res32, gain.reshape(1, d), bias.reshape(1, d))


def _conv_taps(wconv_ref):
    wc = wconv_ref[...]
    return wc[0:1], wc[1:2], wc[2:3]


def _mixer_in_long_kernel(x_ref, wb_ref, wc_ref, wh_ref, wconv_ref, g_ref, st_ref, hist_ref,
                          *, tiles_per_seq, tm):
    m = pl.program_id(1)
    x = x_ref[...]
    b_gate = jnp.dot(x, wb_ref[...], preferred_element_type=F32)
    u = (jnp.dot(x, wc_ref[...], preferred_element_type=F32)
         * jnp.dot(x, wh_ref[...], preferred_element_type=F32))

    @pl.when(m % tiles_per_seq == 0)
    def _():
        hist_ref[0:SUBLANES, :] = jnp.zeros((SUBLANES, hist_ref.shape[1]), F32)

    hist_ref[SUBLANES:SUBLANES + tm, :] = u
    w0, w1, w2 = _conv_taps(wconv_ref)
    conv = (w0 * hist_ref[SUBLANES - 2:SUBLANES - 2 + tm, :]
            + w1 * hist_ref[SUBLANES - 1:SUBLANES - 1 + tm, :]
            + w2 * u)
    g_ref[...] = (b_gate * conv).astype(g_ref.dtype)
    st_ref[0] = u[tm - (CONV_W - 1):tm, :]
    hist_ref[0:SUBLANES, :] = u[tm - SUBLANES:tm, :]


def mixer_in_long(x16, w_in16, w_conv, n_seq, seq_len):
    m_rows, d = n_seq * seq_len, x16.shape[1]
    tm = _pick_tile(seq_len, (512, 256, 128))
    tn = _pick_tile(d, (512, 256, 128))
    n_blocks = d // tn
    tiles_per_seq = seq_len // tm
    kern = functools.partial(_mixer_in_long_kernel, tiles_per_seq=tiles_per_seq, tm=tm)
    return pl.pallas_call(
        kern,
        grid=(n_blocks, m_rows // tm),
        in_specs=[
            pl.BlockSpec((tm, d), lambda n, m: (m, 0)),
            pl.BlockSpec((d, tn), lambda n, m: (0, n)),
            pl.BlockSpec((d, tn), lambda n, m: (0, n + n_blocks)),
            pl.BlockSpec((d, tn), lambda n, m: (0, n + 2 * n_blocks)),
            pl.BlockSpec((CONV_W, tn), lambda n, m: (0, n)),
        ],
        out_specs=[
            pl.BlockSpec((tm, tn), lambda n, m: (m, n)),
            pl.BlockSpec((1, CONV_W - 1, tn), lambda n, m: (m // tiles_per_seq, 0, n)),
        ],
        out_shape=[jax.ShapeDtypeStruct((m_rows, d), BF16),
                   jax.ShapeDtypeStruct((n_seq, CONV_W - 1, d), F32)],
        scratch_shapes=[pltpu.VMEM((tm + SUBLANES, tn), F32)],
        compiler_params=_params(),
        name="mixer_in_long",
    )(x16, w_in16, w_in16, w_in16, w_conv)


def _mixer_in_short_kernel(x_ref, wb_ref, wc_ref, wh_ref, wconv_ref, prev_ref, g_ref, st_ref,
                           *, seq_len):
    x = x_ref[...]
    rows, tn = x.shape[0], wb_ref.shape[1]
    n_seq = rows // seq_len
    b_gate = jnp.dot(x, wb_ref[...], preferred_element_type=F32)
    u = (jnp.dot(x, wc_ref[...], preferred_element_type=F32)
         * jnp.dot(x, wh_ref[...], preferred_element_type=F32))
    u3 = u.reshape(n_seq, seq_len, tn)
    prev = prev_ref[...]
    p0, p1 = prev[:, 0:1, :], prev[:, 1:2, :]
    pos = lax.broadcasted_iota(jnp.int32, u3.shape, 1)
    u1 = jnp.where(pos == 0, p1, pltpu.roll(u3, 1, axis=1))
    u2 = jnp.where(pos == 0, p0, jnp.where(pos == 1, p1, pltpu.roll(u3, 2, axis=1)))
    w0, w1, w2 = _conv_taps(wconv_ref)
    conv = w0 * u2 + w1 * u1 + w2 * u3
    g_ref[...] = (b_gate * conv.reshape(rows, tn)).astype(g_ref.dtype)
    st_ref[...] = u3[:, seq_len - (CONV_W - 1):seq_len, :]


def mixer_in_short(x16, w_in16, w_conv, prev, seq_len):
    m_rows, d = x16.shape
    assert seq_len == SUBLANES
    n_seq = m_rows // seq_len
    tn = _pick_tile(d, (512, 256, 128))
    n_blocks = d // tn
    kern = functools.partial(_mixer_in_short_kernel, seq_len=seq_len)
    return pl.pallas_call(
        kern,
        grid=(n_blocks,),
        in_specs=[
            pl.BlockSpec((m_rows, d), lambda n: (0, 0)),
            pl.BlockSpec((d, tn), lambda n: (0, n)),
            pl.BlockSpec((d, tn), lambda n: (0, n + n_blocks)),
            pl.BlockSpec((d, tn), lambda n: (0, n + 2 * n_blocks)),
            pl.BlockSpec((CONV_W, tn), lambda n: (0, n)),
            pl.BlockSpec((n_seq, CONV_W - 1, tn), lambda n: (0, 0, n)),
        ],
        out_specs=[
            pl.BlockSpec((m_rows, tn), lambda n: (0, n)),
            pl.BlockSpec((n_seq, CONV_W - 1, tn), lambda n: (0, 0, n)),
        ],
        out_shape=[jax.ShapeDtypeStruct((m_rows, d), BF16),
                   jax.ShapeDtypeStruct((n_seq, CONV_W - 1, d), F32)],
        compiler_params=_params(),
        name="mixer_in_short",
    )(x16, w_in16, w_in16, w_in16, w_conv, prev)


def _kv_proj_kernel(x_ref, wk_ref, wv_ref, k32_ref, v32_ref, k16_ref, v16_ref):
    x = x_ref[...]
    k = jnp.dot(x, wk_ref[...], preferred_element_type=F32)
    v = jnp.dot(x, wv_ref[...], preferred_element_type=F32)
    k32_ref[...] = k
    v32_ref[...] = v
    k16_ref[...] = k.astype(BF16)
    v16_ref[...] = v.astype(BF16)


def kv_proj(x16, w_kv16):
    m_rows, d = x16.shape
    tm = _pick_tile(m_rows, (1056, 1024, 512, 256, 128))
    tn = _pick_tile(d, (512, 256, 128))
    n_blocks = d // tn
    out_spec = pl.BlockSpec((tm, tn), lambda n, m: (m, n))
    return pl.pallas_call(
        _kv_proj_kernel,
        grid=(n_blocks, m_rows // tm),
        in_specs=[
            pl.BlockSpec((tm, d), lambda n, m: (m, 0)),
            pl.BlockSpec((d, tn), lambda n, m: (0, n)),
            pl.BlockSpec((d, tn), lambda n, m: (0, n + n_blocks)),
        ],
        out_specs=[out_spec] * 4,
        out_shape=[jax.ShapeDtypeStruct((m_rows, d), F32)] * 2
                  + [jax.ShapeDtypeStruct((m_rows, d), BF16)] * 2,
        compiler_params=_params(),
        name="kv_proj",
    )(x16, w_kv16, w_kv16)


def _q_proj_kernel(x_ref, w_ref, o_ref, *, scale):
    q = jnp.dot(x_ref[...], w_ref[...], preferred_element_type=F32)
    o_ref[...] = (q * scale).astype(o_ref.dtype)


def q_proj(x16, w_q16, scale):
    m_rows, d = x16.shape
    tm = _pick_tile(m_rows, (1056, 1024, 512, 256, 128))
    tn = _pick_tile(d, (512, 256, 128))
    return pl.pallas_call(
        functools.partial(_q_proj_kernel, scale=scale),
        grid=(d // tn, m_rows // tm),
        in_specs=[
            pl.BlockSpec((tm, d), lambda n, m: (m, 0)),
            pl.BlockSpec((d, tn), lambda n, m: (0, n)),
        ],
        out_specs=pl.BlockSpec((tm, tn), lambda n, m: (m, n)),
        out_shape=jax.ShapeDtypeStruct((m_rows, d), BF16),
        compiler_params=_params(),
        name="q_proj",
    )(x16, w_q16)


def _suffix_matrix(n):
    j = lax.broadcasted_iota(jnp.int32, (n, n), 0)
    s = lax.broadcasted_iota(jnp.int32, (n, n), 1)
    return (j > s).astype(BF16)


def _stick_block(z, suffix, carry, causal):
    softplus = jnp.maximum(z, 0.0) + jnp.log1p(jnp.exp(-jnp.abs(z)))
    log_keep = -softplus
    if causal is not None:
        log_keep = jnp.where(causal, log_keep, 0.0)
    hi = log_keep.astype(BF16)
    lo = (log_keep - hi.astype(F32)).astype(BF16)
    after = (jnp.dot(hi, suffix, preferred_element_type=F32)
             + jnp.dot(lo, suffix, preferred_element_type=F32))
    w = jnp.exp((z - softplus) + after + carry)
    if causal is not None:
        w = jnp.where(causal, w, 0.0)
    return w, carry + jnp.sum(log_keep, axis=-1, keepdims=True)


_NT = (((1,), (1,)), ((), ()))


def _sb_prompt_kernel(q_ref, k_ref, v_ref, suffix_ref, o_ref, *, blk):
    qi = pl.program_id(2)
    q = q_ref[...]
    suffix = suffix_ref[...]
    head_dim = q.shape[1]

    def visit(j, carry, acc, causal):
        start = pl.multiple_of(j * blk, blk)
        kj = k_ref[pl.ds(start, blk), :]
        vj = v_ref[pl.ds(start, blk), :]
        z = lax.dot_general(q, kj, _NT, preferred_element_type=F32)
        w, carry = _stick_block(z, suffix, carry, causal)
        return carry, acc + jnp.dot(w.astype(BF16), vj, preferred_element_type=F32)

    t = lax.broadcasted_iota(jnp.int32, (blk, blk), 0)
    s = lax.broadcasted_iota(jnp.int32, (blk, blk), 1)
    carry, acc = visit(qi, jnp.zeros((blk, 1), F32), jnp.zeros((blk, head_dim), F32), s < t)

    def body(i, state):
        return visit(qi - 1 - i, state[0], state[1], None)

    carry, acc = lax.fori_loop(0, qi, body, (carry, acc))
    o_ref[...] = acc.astype(o_ref.dtype)


def sb_attention_prompt(q16, k16, v16, n_seq, seq_len, n_heads, head_dim):
    blk = _pick_tile(seq_len, (256, 128))
    n_q = seq_len // blk
    kern = functools.partial(_sb_prompt_kernel, blk=blk)
    return pl.pallas_call(
        kern,
        grid=(n_seq, n_heads, n_q),
        in_specs=[
            pl.BlockSpec((blk, head_dim), lambda b, h, i: (b * n_q + i, h)),
            pl.BlockSpec((seq_len, head_dim), lambda b, h, i: (b, h)),
            pl.BlockSpec((seq_len, head_dim), lambda b, h, i: (b, h)),
            pl.BlockSpec((blk, blk), lambda b, h, i: (0, 0)),
        ],
        out_specs=pl.BlockSpec((blk, head_dim), lambda b, h, i: (b * n_q + i, h)),
        out_shape=jax.ShapeDtypeStruct((n_seq * seq_len, n_heads * head_dim), BF16),
        compiler_params=_params(),
        name="sb_attention_prompt",
    )(q16, k16, v16, _suffix_matrix(blk))


def _sb_sample_kernel(pt_ref, q_ref, knew_ref, vnew_ref, *rest, pages_per_step, n_heads, dec_seq):
    k_pages = rest[:pages_per_step]
    v_pages = rest[pages_per_step:2 * pages_per_step]
    suffix_ref, o_ref, acc_ref, carry_ref = rest[2 * pages_per_step:]
    del pt_ref
    c = pl.program_id(1)
    head_dim = q_ref.shape[2]
    rows_per_pair = 2 * dec_seq
    suffix = suffix_ref[...]

    def visit(k_head, v_head, causal):
        zs = []
        for hp in range(n_heads // 2):
            lhs = q_ref[0, hp * rows_per_pair:(hp + 1) * rows_per_pair, :]
            for half in range(2):
                z_h = lax.dot_general(lhs, k_head(2 * hp + half), _NT, preferred_element_type=F32)
                zs.append(z_h[half * dec_seq:(half + 1) * dec_seq])
        z = jnp.concatenate(zs, axis=0)
        w, carry = _stick_block(z, suffix, carry_ref[...], causal)
        carry_ref[...] = carry
        w16 = w.astype(BF16)
        outs = []
        for hp in range(n_heads // 2):
            lhs = w16[hp * rows_per_pair:(hp + 1) * rows_per_pair, :]
            for half in range(2):
                o_h = jnp.dot(lhs, v_head(2 * hp + half), preferred_element_type=F32)
                outs.append(o_h[half * dec_seq:(half + 1) * dec_seq])
        acc_ref[...] += jnp.concatenate(outs, axis=0)

    def lane_slab(ref):
        return lambda h: ref[0, :, h * head_dim:(h + 1) * head_dim]

    def strided_rows(ref):
        n_keys = ref.shape[0] // n_heads
        return lambda h: ref[pl.ds(h, n_keys, stride=n_heads), :].astype(BF16)

    @pl.when(c == 0)
    def _():
        acc_ref[...] = jnp.zeros_like(acc_ref)
        carry_ref[...] = jnp.zeros_like(carry_ref)
        n_new = knew_ref.shape[1]
        row = lax.broadcasted_iota(jnp.int32, (n_heads * dec_seq, n_new), 0)
        col = lax.broadcasted_iota(jnp.int32, (n_heads * dec_seq, n_new), 1)
        visit(lane_slab(knew_ref), lane_slab(vnew_ref), col < (row % dec_seq))

    for i in range(pages_per_step):
        visit(strided_rows(k_pages[i]), strided_rows(v_pages[i]), None)

    @pl.when(c == pl.num_programs(1) - 1)
    def _():
        o_ref[0] = acc_ref[...].astype(o_ref.dtype)


def sb_attention_sample(q_hm, knew_pad, vnew_pad, cache_k3, cache_v3, page_table, n_heads, dec_seq):
    n_seq, rows, head_dim = q_hm.shape
    page, width = knew_pad.shape[1:]
    n_pages = page_table.shape[1]
    pages_per_step = _pick_tile(n_pages, (4, 2, 1))
    n_steps = n_pages // pages_per_step

    def page_spec(i):
        def index(b, c, pt):
            return (pt[b, (n_steps - 1 - c) * pages_per_step + (pages_per_step - 1 - i)], 0, 0)
        return pl.BlockSpec((None, page * n_heads, head_dim), index)

    kern = functools.partial(_sb_sample_kernel, pages_per_step=pages_per_step,
                             n_heads=n_heads, dec_seq=dec_seq)
    grid_spec = pltpu.PrefetchScalarGridSpec(
        num_scalar_prefetch=1,
        grid=(n_seq, n_steps),
        in_specs=[
            pl.BlockSpec((1, rows, head_dim), lambda b, c, pt: (b, 0, 0)),
            pl.BlockSpec((1, page, width), lambda b, c, pt: (b, 0, 0)),
            pl.BlockSpec((1, page, width), lambda b, c, pt: (b, 0, 0)),
        ] + [page_spec(i) for i in range(pages_per_step)] * 2
          + [pl.BlockSpec((page, page), lambda b, c, pt: (0, 0))],
        out_specs=pl.BlockSpec((1, rows, head_dim), lambda b, c, pt: (b, 0, 0)),
        scratch_shapes=[pltpu.VMEM((rows, head_dim), F32), pltpu.VMEM((rows, 1), F32)],
    )
    return pl.pallas_call(
        kern,
        grid_spec=grid_spec,
        out_shape=jax.ShapeDtypeStruct((n_seq, rows, head_dim), BF16),
        compiler_params=_params(),
        name="sb_attention_sample",
    )(page_table, q_hm, knew_pad, vnew_pad,
      *([cache_k3] * pages_per_step), *([cache_v3] * pages_per_step), _suffix_matrix(page))


def kernel(x_prompt, x_sample, state_conv, cache_k, cache_v, page_table, ln_g, ln_b, w_ffn_in,
           w_ffn_out, w_a_in, w_a_conv, w_a_out, w_kv, w_q, w_o):
    n_p, seq, d = x_prompt.shape
    n_s, dec_seq, _ = x_sample.shape
    depth = ln_g.shape[0]
    n_a = w_a_in.shape[0]
    n_phys, page, n_heads, head_dim = cache_k.shape
    rows_p, rows_s = n_p * seq, n_s * dec_seq
    alpha = (2.0 * depth) ** 0.25
    assert n_a == 1 and depth == 2, "one short-conv layer followed by one attention layer"

    x32 = jnp.concatenate([x_prompt.reshape(rows_p, d), x_sample.reshape(rows_s, d)], axis=0)
    x16 = x32.astype(BF16)

    def ffn(x32, x16, layer, half, ln_idx):
        h = swiglu_in(x16, w_ffn_in[layer, half].astype(BF16))
        return proj_res_ln(h, w_ffn_out[layer, half].astype(BF16), x32,
                           ln_g[layer, ln_idx], ln_b[layer, ln_idx], alpha=alpha, coef=0.5)

    x32, x16 = ffn(x32, x16, 0, 0, 0)
    w_in16 = w_a_in[0].astype(BF16)
    g_p, conv_prompt = mixer_in_long(x16, w_in16, w_a_conv[0], n_p, seq)
    g_s, conv_sample = mixer_in_short(x16[rows_p:], w_in16, w_a_conv[0], state_conv[0], dec_seq)
    x32, x16 = proj_res_ln(jnp.concatenate([g_p, g_s], axis=0), w_a_out[0].astype(BF16), x32,
                           ln_g[0, 1], ln_b[0, 1], alpha=alpha, coef=1.0)
    x32, x16 = ffn(x32, x16, 0, 1, 2)

    k32, v32, k16, v16 = kv_proj(x16, w_kv.astype(BF16))

    x32, x16 = ffn(x32, x16, 1, 0, 0)
    q16 = q_proj(x16, w_q[0].astype(BF16), head_dim ** -0.5)
    o_p = sb_attention_prompt(q16, k16, v16, n_p, seq, n_heads, head_dim)

    q_hm = (q16[rows_p:].reshape(n_s, dec_seq, n_heads, head_dim)
            .transpose(0, 2, 1, 3).reshape(n_s, n_heads * dec_seq, head_dim))
    pad = ((0, 0), (0, page - dec_seq), (0, 0))
    knew_pad = jnp.pad(k16[rows_p:].reshape(n_s, dec_seq, d), pad)
    vnew_pad = jnp.pad(v16[rows_p:].reshape(n_s, dec_seq, d), pad)
    o_hm = sb_attention_sample(q_hm, knew_pad, vnew_pad,
                               cache_k.reshape(n_phys, page * n_heads, head_dim),
                               cache_v.reshape(n_phys, page * n_heads, head_dim),
                               page_table, n_heads, dec_seq)
    o_s = (o_hm.reshape(n_s, n_heads, dec_seq, head_dim).transpose(0, 2, 1, 3).reshape(rows_s, d))

    x32, x16 = proj_res_ln(jnp.concatenate([o_p, o_s], axis=0), w_o[0].astype(BF16), x32,
                           ln_g[1, 1], ln_b[1, 1], alpha=alpha, coef=1.0)
    x32, x16 = ffn(x32, x16, 1, 1, 2)

    kv_shape_p = (n_p, seq, n_heads, head_dim)
    kv_shape_s = (n_s, dec_seq, n_heads, head_dim)
    return (x32[:rows_p].reshape(n_p, seq, d), x32[rows_p:].reshape(n_s, dec_seq, d),
            conv_prompt[None], conv_sample[None],
            k32[:rows_p].reshape(kv_shape_p), v32[:rows_p].reshape(kv_shape_p),
            k32[rows_p:].reshape(kv_shape_s), v32[rows_p:].reshape(kv_shape_s))
```

```python
import functools
import math

import jax
import jax.numpy as jnp
from jax import lax
from jax.experimental import pallas as pl
from jax.experimental.pallas import tpu as pltpu

F32 = jnp.float32
BF16 = jnp.bfloat16

LN_EPS = 1e-5
CONV_W = 3
SUBLANES = 8
VMEM_LIMIT_BYTES = 56 * 1024 * 1024
F32_SIGN_BIT = 0x80000000


def _params():
    return pltpu.CompilerParams(vmem_limit_bytes=VMEM_LIMIT_BYTES)


def _pick_tile(total, candidates):
    for c in candidates:
        if total % c == 0:
            return c
    return total


def _cast_once(step, pairs):
    @pl.when(step == 0)
    def _():
        for src, dst in pairs:
            dst[...] = src[...].astype(BF16)


def _cast_kernel(w_ref, o_ref):
    o_ref[...] = w_ref[...].astype(BF16)


def cast_weight(w_stack, lead):
    k_dim, n_dim = w_stack.shape[-2:]
    tk = _pick_tile(k_dim, (512, 256, 128))
    nones = (None,) * len(lead)
    return pl.pallas_call(
        _cast_kernel,
        grid=(k_dim // tk,),
        in_specs=[pl.BlockSpec(nones + (tk, n_dim), lambda k: lead + (k, 0))],
        out_specs=pl.BlockSpec((tk, n_dim), lambda k: (k, 0)),
        out_shape=jax.ShapeDtypeStruct((k_dim, n_dim), BF16),
        compiler_params=_params(),
        name="cast_weight",
    )(w_stack)


def _swiglu_in_kernel(x_ref, wa_ref, wg_ref, o_ref, wa16_ref, wg16_ref):
    _cast_once(pl.program_id(1), ((wa_ref, wa16_ref), (wg_ref, wg16_ref)))
    x = x_ref[...]
    a = jnp.dot(x, wa16_ref[...], preferred_element_type=F32)
    g = jnp.dot(x, wg16_ref[...], preferred_element_type=F32)
    o_ref[...] = (a * jax.nn.sigmoid(a) * g).astype(o_ref.dtype)


def swiglu_in(x16, w_ffn_in, layer, half):
    m_rows, d = x16.shape
    f = w_ffn_in.shape[-1] // 2
    tm = _pick_tile(m_rows, (1056, 1024, 512, 256, 128))
    tn = _pick_tile(f, (512, 256, 128))
    n_blocks = f // tn
    return pl.pallas_call(
        _swiglu_in_kernel,
        grid=(n_blocks, m_rows // tm),
        in_specs=[
            pl.BlockSpec((tm, d), lambda n, m: (m, 0)),
            pl.BlockSpec((None, None, d, tn), lambda n, m: (layer, half, 0, n)),
            pl.BlockSpec((None, None, d, tn), lambda n, m: (layer, half, 0, n + n_blocks)),
        ],
        out_specs=pl.BlockSpec((tm, tn), lambda n, m: (m, n)),
        out_shape=jax.ShapeDtypeStruct((m_rows, f), BF16),
        scratch_shapes=[pltpu.VMEM((d, tn), BF16)] * 2,
        compiler_params=_params(),
        name="swiglu_in",
    )(x16, w_ffn_in, w_ffn_in)


def _proj_res_ln_kernel(a_ref, w_ref, res_ref, gain_ref, bias_ref, *rest, alpha, coef, n_k):
    out_refs, acc_ref = rest[:-1], rest[-1]
    k = pl.program_id(1)
    part = jnp.dot(a_ref[...], w_ref[...], preferred_element_type=F32)

    if n_k > 1:
        @pl.when(k == 0)
        def _():
            acc_ref[...] = part

        @pl.when(jnp.logical_and(k > 0, k < n_k - 1))
        def _():
            acc_ref[...] += part

    @pl.when(k == n_k - 1)
    def _():
        total = part if n_k == 1 else acc_ref[...] + part
        y = alpha * res_ref[...] + coef * total
        mu = jnp.mean(y, axis=-1, keepdims=True)
        dev = y - mu
        var = jnp.mean(dev * dev, axis=-1, keepdims=True)
        out = dev * lax.rsqrt(var + LN_EPS) * gain_ref[...] + bias_ref[...]
        for o_ref in out_refs:
            o_ref[...] = out.astype(o_ref.dtype)


def proj_res_ln(a16, w16, res32, gain, bias, *, alpha, coef, row_start=0, n_rows=None,
                out_dtypes=(F32, BF16)):
    k_dim = a16.shape[1]
    d = w16.shape[1]
    n_rows = a16.shape[0] if n_rows is None else n_rows
    tm = _pick_tile(math.gcd(n_rows, row_start) if row_start else n_rows, (528, 512, 256, 128))
    tk = k_dim if k_dim <= 2048 else _pick_tile(k_dim, (1408, 1024, 512, 256, 128))
    n_k = k_dim // tk
    m0 = row_start // tm
    kern = functools.partial(_proj_res_ln_kernel, alpha=alpha, coef=coef, n_k=n_k)
    return pl.pallas_call(
        kern,
        grid=(n_rows // tm, n_k),
        in_specs=[
            pl.BlockSpec((tm, tk), lambda m, k: (m + m0, k)),
            pl.BlockSpec((tk, d), lambda m, k: (k, 0)),
            pl.BlockSpec((tm, d), lambda m, k: (m + m0, 0)),
            pl.BlockSpec((1, d), lambda m, k: (0, 0)),
            pl.BlockSpec((1, d), lambda m, k: (0, 0)),
        ],
        out_specs=[pl.BlockSpec((tm, d), lambda m, k: (m, 0)) for _ in out_dtypes],
        out_shape=[jax.ShapeDtypeStruct((n_rows, d), dt) for dt in out_dtypes],
        scratch_shapes=[pltpu.VMEM((tm, d), F32)],
        compiler_params=_params(),
        name="proj_res_ln",
    )(a16, w16, res32, gain.reshape(1, d), bias.reshape(1, d))


def _conv_taps(wconv_ref):
    wc = wconv_ref[...]
    return wc[0:1], wc[1:2], wc[2:3]


def _gates(x_ref, w16_refs):
    x = x_ref[...]
    b_gate, c_gate, h = (jnp.dot(x, w[...], preferred_element_type=F32) for w in w16_refs)
    return b_gate, c_gate * h


def _mixer_in_long_kernel(x_ref, wb_ref, wc_ref, wh_ref, wconv_ref, g_ref, st_ref,
                          wb16_ref, wc16_ref, wh16_ref, hist_ref, *, tiles_per_seq, tm):
    m = pl.program_id(1)
    w16 = (wb16_ref, wc16_ref, wh16_ref)
    _cast_once(m, zip((wb_ref, wc_ref, wh_ref), w16))
    b_gate, u = _gates(x_ref, w16)

    @pl.when(m % tiles_per_seq == 0)
    def _():
        hist_ref[0:SUBLANES, :] = jnp.zeros((SUBLANES, hist_ref.shape[1]), F32)

    hist_ref[SUBLANES:SUBLANES + tm, :] = u
    w0, w1, w2 = _conv_taps(wconv_ref)
    conv = (w0 * hist_ref[SUBLANES - 2:SUBLANES - 2 + tm, :]
            + w1 * hist_ref[SUBLANES - 1:SUBLANES - 1 + tm, :]
            + w2 * u)
    g_ref[...] = (b_gate * conv).astype(g_ref.dtype)
    st_ref[0] = u[tm - (CONV_W - 1):tm, :]
    hist_ref[0:SUBLANES, :] = u[tm - SUBLANES:tm, :]


def _mixer_w_specs(d, tn, n_blocks, index):
    return [pl.BlockSpec((None, d, tn), index(j * n_blocks)) for j in range(3)]


def mixer_in_long(x16, w_a_in, w_a_conv, layer, n_seq, seq_len):
    d = x16.shape[1]
    tm = _pick_tile(seq_len, (512, 256, 128))
    tn = _pick_tile(d, (512, 256, 128))
    n_blocks = d // tn
    tiles_per_seq = seq_len // tm
    kern = functools.partial(_mixer_in_long_kernel, tiles_per_seq=tiles_per_seq, tm=tm)
    return pl.pallas_call(
        kern,
        grid=(n_blocks, n_seq * tiles_per_seq),
        in_specs=[pl.BlockSpec((tm, d), lambda n, m: (m, 0))]
        + _mixer_w_specs(d, tn, n_blocks, lambda off: (lambda n, m: (layer, 0, n + off)))
        + [pl.BlockSpec((None, CONV_W, tn), lambda n, m: (layer, 0, n))],
        out_specs=[
            pl.BlockSpec((tm, tn), lambda n, m: (m, n)),
            pl.BlockSpec((1, CONV_W - 1, tn), lambda n, m: (m // tiles_per_seq, 0, n)),
        ],
        out_shape=[jax.ShapeDtypeStruct((n_seq * seq_len, d), BF16),
                   jax.ShapeDtypeStruct((n_seq, CONV_W - 1, d), F32)],
        scratch_shapes=[pltpu.VMEM((d, tn), BF16)] * 3 + [pltpu.VMEM((tm + SUBLANES, tn), F32)],
        compiler_params=_params(),
        name="mixer_in_long",
    )(x16, w_a_in, w_a_in, w_a_in, w_a_conv)


def _mixer_in_short_kernel(x_ref, wb_ref, wc_ref, wh_ref, wconv_ref, prev_ref, g_ref, st_ref,
                           *, seq_len):
    rows, tn = x_ref.shape[0], wb_ref.shape[1]
    n_seq = rows // seq_len
    x = x_ref[...]
    b_gate, c_gate, h = (jnp.dot(x, w[...].astype(BF16), preferred_element_type=F32)
                         for w in (wb_ref, wc_ref, wh_ref))
    u3 = (c_gate * h).reshape(n_seq, seq_len, tn)
    prev = prev_ref[...]
    p0, p1 = prev[:, 0:1, :], prev[:, 1:2, :]
    pos = lax.broadcasted_iota(jnp.int32, u3.shape, 1)
    u1 = jnp.where(pos == 0, p1, pltpu.roll(u3, 1, axis=1))
    u2 = jnp.where(pos == 0, p0, jnp.where(pos == 1, p1, pltpu.roll(u3, 2, axis=1)))
    w0, w1, w2 = _conv_taps(wconv_ref)
    conv = w0 * u2 + w1 * u1 + w2 * u3
    g_ref[...] = (b_gate * conv.reshape(rows, tn)).astype(g_ref.dtype)
    st_ref[...] = u3[:, seq_len - (CONV_W - 1):seq_len, :]


def mixer_in_short(x16, w_a_in, w_a_conv, prev, layer, row_start, n_seq, seq_len):
    d = x16.shape[1]
    rows = n_seq * seq_len
    assert seq_len == SUBLANES and row_start % rows == 0
    tn = _pick_tile(d, (512, 256, 128))
    n_blocks = d // tn
    kern = functools.partial(_mixer_in_short_kernel, seq_len=seq_len)
    return pl.pallas_call(
        kern,
        grid=(n_blocks,),
        in_specs=[pl.BlockSpec((rows, d), lambda n: (row_start // rows, 0))]
        + _mixer_w_specs(d, tn, n_blocks, lambda off: (lambda n: (layer, 0, n + off)))
        + [pl.BlockSpec((None, CONV_W, tn), lambda n: (layer, 0, n)),
           pl.BlockSpec((None, n_seq, CONV_W - 1, tn), lambda n: (layer, 0, 0, n))],
        out_specs=[
            pl.BlockSpec((rows, tn), lambda n: (0, n)),
            pl.BlockSpec((n_seq, CONV_W - 1, tn), lambda n: (0, 0, n)),
        ],
        out_shape=[jax.ShapeDtypeStruct((rows, d), BF16),
                   jax.ShapeDtypeStruct((n_seq, CONV_W - 1, d), F32)],
        compiler_params=_params(),
        name="mixer_in_short",
    )(x16, w_a_in, w_a_in, w_a_in, w_a_conv, prev)


def _kv_proj_kernel(x_ref, wk_ref, wv_ref, k32_ref, v32_ref, k16_ref, v16_ref, *, head_dim):
    x = x_ref[...]
    tm, heads = k32_ref.shape[:2]
    for w_ref, o32_ref, o16_ref in ((wk_ref, k32_ref, k16_ref), (wv_ref, v32_ref, v16_ref)):
        y = jnp.dot(x, w_ref[...], preferred_element_type=F32)
        o16_ref[...] = y.astype(BF16)
        rows = o32_ref.reshape(tm * heads, head_dim)
        for h in range(heads):
            rows[pl.ds(h, tm, stride=heads), :] = y[:, h * head_dim:(h + 1) * head_dim]


def kv_proj(x16, w_kv16, row_start, n_rows, n_heads, head_dim):
    d = x16.shape[1]
    tm = _pick_tile(math.gcd(n_rows, row_start) if row_start else n_rows, (512, 256, 128))
    heads_per_block = SUBLANES
    tn = heads_per_block * head_dim
    n_blocks = d // tn
    m0 = row_start // tm
    spec32 = pl.BlockSpec((tm, heads_per_block, head_dim), lambda n, m: (m, n, 0))
    spec16 = pl.BlockSpec((tm, tn), lambda n, m: (m, n))
    return pl.pallas_call(
        functools.partial(_kv_proj_kernel, head_dim=head_dim),
        grid=(n_blocks, n_rows // tm),
        in_specs=[
            pl.BlockSpec((tm, d), lambda n, m: (m + m0, 0)),
            pl.BlockSpec((d, tn), lambda n, m: (0, n)),
            pl.BlockSpec((d, tn), lambda n, m: (0, n + n_blocks)),
        ],
        out_specs=[spec32, spec32, spec16, spec16],
        out_shape=[jax.ShapeDtypeStruct((n_rows, n_heads, head_dim), F32)] * 2
                  + [jax.ShapeDtypeStruct((n_rows, d), BF16)] * 2,
        compiler_params=_params(),
        name="kv_proj",
    )(x16, w_kv16, w_kv16)


def _q_proj_kernel(x_ref, w_ref, o_ref, w16_ref, *, scale):
    _cast_once(pl.program_id(1), ((w_ref, w16_ref),))
    q = jnp.dot(x_ref[...], w16_ref[...], preferred_element_type=F32)
    o_ref[...] = (q * scale).astype(o_ref.dtype)


def q_proj(x16, w_q, layer, scale):
    m_rows, d = x16.shape
    tm = _pick_tile(m_rows, (1056, 1024, 512, 256, 128))
    tn = _pick_tile(d, (512, 256, 128))
    return pl.pallas_call(
        functools.partial(_q_proj_kernel, scale=scale),
        grid=(d // tn, m_rows // tm),
        in_specs=[
            pl.BlockSpec((tm, d), lambda n, m: (m, 0)),
            pl.BlockSpec((None, d, tn), lambda n, m: (layer, 0, n)),
        ],
        out_specs=pl.BlockSpec((tm, tn), lambda n, m: (m, n)),
        out_shape=jax.ShapeDtypeStruct((m_rows, d), BF16),
        scratch_shapes=[pltpu.VMEM((d, tn), BF16)],
        compiler_params=_params(),
        name="q_proj",
    )(x16, w_q)


def _suffix_matrix(n):
    j = lax.broadcasted_iota(jnp.int32, (2 * n, n), 0) % n
    s = lax.broadcasted_iota(jnp.int32, (2 * n, n), 1)
    return (j > s).astype(BF16)


def _neg_abs(x):
    return pltpu.bitcast(pltpu.bitcast(x, jnp.uint32) | jnp.uint32(F32_SIGN_BIT), F32)


def _log2_keep(zn, causal):
    lk = jnp.minimum(zn, 0.0) - jnp.log2(1.0 + jnp.exp2(_neg_abs(zn)))
    return lk if causal is None else jnp.where(causal, lk, 0.0)


def _stick_weights(zn, lk, suffix, carry):
    hi = lk.astype(BF16)
    lo = (lk - hi.astype(F32)).astype(BF16)
    after = jnp.dot(jnp.concatenate([hi, lo], axis=1), suffix, preferred_element_type=F32)
    w = jnp.exp2((lk - zn) + after + carry)
    return w, carry + jnp.sum(lk, axis=-1, keepdims=True)


_NT = (((1,), (1,)), ((), ()))


def _sb_prompt_kernel(q_ref, k_ref, v_ref, suffix_ref, o_ref, *, tq, sub, n_sub):
    qi = pl.program_id(2)
    q = q_ref[...]
    suffix = suffix_ref[...]
    head_dim = q.shape[1]
    step = sub * n_sub

    def visit(key_start, carry, acc, causal):
        k = k_ref[pl.ds(key_start, step), :]
        v = v_ref[pl.ds(key_start, step), :]
        zn = lax.dot_general(q, k, _NT, preferred_element_type=F32)
        lk = _log2_keep(zn, causal)
        ws = [None] * n_sub
        for s in reversed(range(n_sub)):
            cols = slice(s * sub, (s + 1) * sub)
            ws[s], carry = _stick_weights(zn[:, cols], lk[:, cols], suffix, carry)
        w = jnp.concatenate(ws, axis=1)
        if causal is not None:
            w = jnp.where(causal, w, 0.0)
        return carry, acc + jnp.dot(w.astype(BF16), v, preferred_element_type=F32)

    carry = jnp.zeros((tq, 1), F32)
    acc = jnp.zeros((tq, head_dim), F32)
    t = lax.broadcasted_iota(jnp.int32, (tq, step), 0)
    s = lax.broadcasted_iota(jnp.int32, (tq, step), 1)
    n_diag = tq // step
    for d in reversed(range(n_diag)):
        carry, acc = visit(pl.multiple_of(qi * tq + d * step, step), carry, acc, s + d * step < t)

    def body(i, state):
        return visit(pl.multiple_of((qi * n_diag - 1 - i) * step, step), state[0], state[1], None)

    carry, acc = lax.fori_loop(0, qi * n_diag, body, (carry, acc))
    o_ref[...] = acc.astype(o_ref.dtype)


def sb_attention_prompt(q16, k16, v16, n_seq, seq_len, n_heads, head_dim):
    tq = _pick_tile(seq_len, (512, 256, 128))
    sub = min(256, tq)
    n_sub = tq // sub
    n_q = seq_len // tq
    kern = functools.partial(_sb_prompt_kernel, tq=tq, sub=sub, n_sub=n_sub)
    return pl.pallas_call(
        kern,
        grid=(n_seq, n_heads, n_q),
        in_specs=[
            pl.BlockSpec((tq, head_dim), lambda b, h, i: (b * n_q + i, h)),
            pl.BlockSpec((seq_len, head_dim), lambda b, h, i: (b, h)),
            pl.BlockSpec((seq_len, head_dim), lambda b, h, i: (b, h)),
            pl.BlockSpec((2 * sub, sub), lambda b, h, i: (0, 0)),
        ],
        out_specs=pl.BlockSpec((tq, head_dim), lambda b, h, i: (b * n_q + i, h)),
        out_shape=jax.ShapeDtypeStruct(k16.shape, BF16),
        compiler_params=_params(),
        name="sb_attention_prompt",
    )(q16, k16, v16, _suffix_matrix(sub))


def _sb_sample_kernel(pt_ref, q_ref, knew_ref, vnew_ref, *rest, pages_per_step, n_heads, dec_seq):
    n_blocks = 2 * pages_per_step
    k_pages, v_pages = rest[:n_blocks], rest[n_blocks:2 * n_blocks]
    suffix_ref, o_ref, acc_ref, carry_ref = rest[2 * n_blocks:]
    del pt_ref
    c = pl.program_id(1)
    head_dim = q_ref.shape[2]
    half_heads = n_heads // 2
    rows_per_pair = 2 * dec_seq
    suffix = suffix_ref[...]

    def visit(k_head, v_head, causal):
        zs = []
        for hp in range(n_heads // 2):
            lhs = q_ref[0, hp * rows_per_pair:(hp + 1) * rows_per_pair, :]
            for half in range(2):
                z_h = lax.dot_general(lhs, k_head(2 * hp + half), _NT, preferred_element_type=F32)
                zs.append(z_h[half * dec_seq:(half + 1) * dec_seq])
        zn = jnp.concatenate(zs, axis=0)
        w, carry = _stick_weights(zn, _log2_keep(zn, causal), suffix, carry_ref[...])
        if causal is not None:
            w = jnp.where(causal, w, 0.0)
        carry_ref[...] = carry
        w16 = w.astype(BF16)
        outs = []
        for hp in range(n_heads // 2):
            lhs = w16[hp * rows_per_pair:(hp + 1) * rows_per_pair, :]
            for half in range(2):
                o_h = jnp.dot(lhs, v_head(2 * hp + half), preferred_element_type=F32)
                outs.append(o_h[half * dec_seq:(half + 1) * dec_seq])
        acc_ref[...] += jnp.concatenate(outs, axis=0)

    def lane_slab(ref):
        return lambda h: ref[0, :, h * head_dim:(h + 1) * head_dim]

    def strided_rows(ref_lo, ref_hi):
        def get(h):
            ref = ref_lo if h < half_heads else ref_hi
            n_keys = ref.shape[0]
            rows = ref.reshape(n_keys * half_heads, head_dim)
            return rows[pl.ds(h % half_heads, n_keys, stride=half_heads), :].astype(BF16)
        return get

    @pl.when(c == 0)
    def _():
        acc_ref[...] = jnp.zeros_like(acc_ref)
        carry_ref[...] = jnp.zeros_like(carry_ref)
        n_new = knew_ref.shape[1]
        row = lax.broadcasted_iota(jnp.int32, (n_heads * dec_seq, n_new), 0)
        col = lax.broadcasted_iota(jnp.int32, (n_heads * dec_seq, n_new), 1)
        visit(lane_slab(knew_ref), lane_slab(vnew_ref), col < (row % dec_seq))

    for i in range(pages_per_step):
        visit(strided_rows(k_pages[2 * i], k_pages[2 * i + 1]),
              strided_rows(v_pages[2 * i], v_pages[2 * i + 1]), None)

    @pl.when(c == pl.num_programs(1) - 1)
    def _():
        o_ref[0] = acc_ref[...].astype(o_ref.dtype)


def sb_attention_sample(q_hm, knew_pad, vnew_pad, cache_k, cache_v, page_table, dec_seq):
    n_seq, rows, head_dim = q_hm.shape
    n_phys, page, n_heads, _ = cache_k.shape
    n_pages = page_table.shape[1]
    pages_per_step = _pick_tile(n_pages, (2, 1))
    n_steps = n_pages // pages_per_step
    assert n_heads % (2 * SUBLANES) == 0
    half_heads = n_heads // 2

    def page_spec(i, half):
        def index(b, c, pt):
            return (pt[b, (n_steps - 1 - c) * pages_per_step + (pages_per_step - 1 - i)], half, 0)
        return pl.BlockSpec((page, half_heads, head_dim), index)

    page_specs = [page_spec(i, half) for i in range(pages_per_step) for half in range(2)]
    kern = functools.partial(_sb_sample_kernel, pages_per_step=pages_per_step,
                             n_heads=n_heads, dec_seq=dec_seq)
    grid_spec = pltpu.PrefetchScalarGridSpec(
        num_scalar_prefetch=1,
        grid=(n_seq, n_steps),
        in_specs=[
            pl.BlockSpec((1, rows, head_dim), lambda b, c, pt: (b, 0, 0)),
            pl.BlockSpec((1, page, n_heads * head_dim), lambda b, c, pt: (b, 0, 0)),
            pl.BlockSpec((1, page, n_heads * head_dim), lambda b, c, pt: (b, 0, 0)),
        ] + page_specs * 2 + [pl.BlockSpec((2 * page, page), lambda b, c, pt: (0, 0))],
        out_specs=pl.BlockSpec((1, rows, head_dim), lambda b, c, pt: (b, 0, 0)),
        scratch_shapes=[pltpu.VMEM((rows, head_dim), F32), pltpu.VMEM((rows, 1), F32)],
    )
    ck = cache_k.reshape(n_phys * page, n_heads, head_dim)
    cv = cache_v.reshape(n_phys * page, n_heads, head_dim)
    return pl.pallas_call(
        kern,
        grid_spec=grid_spec,
        out_shape=jax.ShapeDtypeStruct((n_seq, rows, head_dim), BF16),
        compiler_params=_params(),
        name="sb_attention_sample",
    )(page_table, q_hm, knew_pad, vnew_pad, *([ck] * len(page_specs)), *([cv] * len(page_specs)),
      _suffix_matrix(page))


def kernel(x_prompt, x_sample, state_conv, cache_k, cache_v, page_table, ln_g, ln_b, w_ffn_in,
           w_ffn_out, w_a_in, w_a_conv, w_a_out, w_kv, w_q, w_o):
    n_p, seq, d = x_prompt.shape
    n_s, dec_seq, _ = x_sample.shape
    depth = ln_g.shape[0]
    page, n_heads, head_dim = cache_k.shape[1:]
    rows_p, rows_s = n_p * seq, n_s * dec_seq
    alpha = (2.0 * depth) ** 0.25
    assert w_a_in.shape[0] == 1 and depth == 2, "one short-conv layer followed by one attention layer"

    x32 = jnp.concatenate([x_prompt.reshape(rows_p, d), x_sample.reshape(rows_s, d)], axis=0)
    x16 = x32.astype(BF16)

    def ffn(x32, x16, layer, half, ln_idx, row_splits=None):
        h = swiglu_in(x16, w_ffn_in, layer, half)
        w_out16 = cast_weight(w_ffn_out, (layer, half))
        norm = functools.partial(proj_res_ln, h, w_out16, x32, ln_g[layer, ln_idx], ln_b[layer, ln_idx],
                                 alpha=alpha, coef=0.5)
        if row_splits is None:
            return norm()
        return [norm(row_start=r0, n_rows=n, out_dtypes=(F32,))[0] for r0, n in row_splits]

    x32, x16 = ffn(x32, x16, 0, 0, 0)
    g, conv_prompt = mixer_in_long(x16, w_a_in, w_a_conv, 0, n_p, seq)
    g_s, conv_sample = mixer_in_short(x16, w_a_in, w_a_conv, state_conv, 0, rows_p, n_s, dec_seq)
    x32, x16 = proj_res_ln(jnp.concatenate([g, g_s], axis=0), cast_weight(w_a_out, (0,)), x32, ln_g[0, 1], ln_b[0, 1],
                           alpha=alpha, coef=1.0)
    x32, x16 = ffn(x32, x16, 0, 1, 2)

    w_kv16 = w_kv.astype(BF16)
    k_prompt, v_prompt, k16_p, v16_p = kv_proj(x16, w_kv16, 0, rows_p, n_heads, head_dim)
    k_sample, v_sample, k16_s, v16_s = kv_proj(x16, w_kv16, rows_p, rows_s, n_heads, head_dim)

    x32, x16 = ffn(x32, x16, 1, 0, 0)
    q16 = q_proj(x16, w_q, 0, -math.log2(math.e) * head_dim ** -0.5)
    o = sb_attention_prompt(q16, k16_p, v16_p, n_p, seq, n_heads, head_dim)

    q_hm = (q16[rows_p:].reshape(n_s, dec_seq, n_heads, head_dim)
            .transpose(0, 2, 1, 3).reshape(n_s, n_heads * dec_seq, head_dim))
    pad = ((0, 0), (0, page - dec_seq), (0, 0))
    o_hm = sb_attention_sample(q_hm, jnp.pad(k16_s.reshape(n_s, dec_seq, d), pad),
                               jnp.pad(v16_s.reshape(n_s, dec_seq, d), pad),
                               cache_k, cache_v, page_table, dec_seq)
    o_s = o_hm.reshape(n_s, n_heads, dec_seq, head_dim).transpose(0, 2, 1, 3).reshape(rows_s, d)

    x32, x16 = proj_res_ln(jnp.concatenate([o, o_s], axis=0), cast_weight(w_o, (0,)), x32, ln_g[1, 1], ln_b[1, 1],
                           alpha=alpha, coef=1.0)
    y_prompt, y_sample = ffn(x32, x16, 1, 1, 2, row_splits=((0, rows_p), (rows_p, rows_s)))

    return (y_prompt.reshape(n_p, seq, d), y_sample.reshape(n_s, dec_seq, d),
            conv_prompt[None], conv_sample[None],
            k_prompt.reshape(n_p, seq, n_heads, head_dim), v_prompt.reshape(n_p, seq, n_heads, head_dim),
            k_sample.reshape(n_s, dec_seq, n_heads, head_dim),
            v_sample.reshape(n_s, dec_seq, n_heads, head_dim))
```

```python
import functools
import math

import jax
import jax.numpy as jnp
from jax import lax
from jax.experimental import pallas as pl
from jax.experimental.pallas import tpu as pltpu

F32 = jnp.float32
BF16 = jnp.bfloat16

LN_EPS = 1e-5
CONV_W = 3
SUBLANES = 8
VMEM_LIMIT_BYTES = 56 * 1024 * 1024


def _params():
    return pltpu.CompilerParams(vmem_limit_bytes=VMEM_LIMIT_BYTES)


def _pick_tile(total, candidates):
    for c in candidates:
        if total % c == 0:
            return c
    return total


def _cast_once(step, pairs):
    @pl.when(step == 0)
    def _():
        for src, dst in pairs:
            dst[...] = src[...].astype(BF16)


def _cast_kernel(w_ref, o_ref):
    o_ref[...] = w_ref[...].astype(BF16)


def cast_weight(w_stack, lead):
    k_dim, n_dim = w_stack.shape[-2:]
    tk = _pick_tile(k_dim, (512, 256, 128))
    nones = (None,) * len(lead)
    return pl.pallas_call(
        _cast_kernel,
        grid=(k_dim // tk,),
        in_specs=[pl.BlockSpec(nones + (tk, n_dim), lambda k: lead + (k, 0))],
        out_specs=pl.BlockSpec((tk, n_dim), lambda k: (k, 0)),
        out_shape=jax.ShapeDtypeStruct((k_dim, n_dim), BF16),
        compiler_params=_params(),
        name="cast_weight",
    )(w_stack)


def _swiglu_in_kernel(x_ref, wa_ref, wg_ref, o_ref, wa16_ref, wg16_ref):
    _cast_once(pl.program_id(1), ((wa_ref, wa16_ref), (wg_ref, wg16_ref)))
    x = x_ref[...]
    a = jnp.dot(x, wa16_ref[...], preferred_element_type=F32)
    g = jnp.dot(x, wg16_ref[...], preferred_element_type=F32)
    o_ref[...] = (a * jax.nn.sigmoid(a) * g).astype(o_ref.dtype)


def swiglu_in(x16, w_ffn_in, layer, half):
    m_rows, d = x16.shape
    f = w_ffn_in.shape[-1] // 2
    tm = _pick_tile(m_rows, (1056, 1024, 512, 256, 128))
    tn = _pick_tile(f, (512, 256, 128))
    n_blocks = f // tn
    return pl.pallas_call(
        _swiglu_in_kernel,
        grid=(n_blocks, m_rows // tm),
        in_specs=[
            pl.BlockSpec((tm, d), lambda n, m: (m, 0)),
            pl.BlockSpec((None, None, d, tn), lambda n, m: (layer, half, 0, n)),
            pl.BlockSpec((None, None, d, tn), lambda n, m: (layer, half, 0, n + n_blocks)),
        ],
        out_specs=pl.BlockSpec((tm, tn), lambda n, m: (m, n)),
        out_shape=jax.ShapeDtypeStruct((m_rows, f), BF16),
        scratch_shapes=[pltpu.VMEM((d, tn), BF16)] * 2,
        compiler_params=_params(),
        name="swiglu_in",
    )(x16, w_ffn_in, w_ffn_in)


def _proj_res_ln_kernel(a_ref, w_ref, res_ref, gain_ref, bias_ref, *out_refs, alpha, coef, chunk):
    for r in range(a_ref.shape[0] // chunk):
        rows = slice(r * chunk, (r + 1) * chunk)
        y = alpha * res_ref[rows, :] + coef * jnp.dot(a_ref[rows, :], w_ref[...],
                                                      preferred_element_type=F32)
        mu = jnp.mean(y, axis=-1, keepdims=True)
        dev = y - mu
        var = jnp.mean(dev * dev, axis=-1, keepdims=True)
        out = dev * lax.rsqrt(var + LN_EPS) * gain_ref[...] + bias_ref[...]
        for o_ref in out_refs:
            o_ref[rows, :] = out.astype(o_ref.dtype)


def proj_res_ln(a16, w16, res32, gain, bias, *, alpha, coef, row_start=0, n_rows=None,
                out_dtypes=(F32, BF16)):
    k_dim = a16.shape[1]
    d = w16.shape[1]
    n_rows = a16.shape[0] if n_rows is None else n_rows
    tiles = (528, 512, 256, 128) if k_dim <= d else (384, 256, 128)
    tm = _pick_tile(math.gcd(n_rows, row_start) if row_start else n_rows, tiles)
    chunk = _pick_tile(tm, (192, 176, 128))
    m0 = row_start // tm
    kern = functools.partial(_proj_res_ln_kernel, alpha=alpha, coef=coef, chunk=chunk)
    return pl.pallas_call(
        kern,
        grid=(n_rows // tm,),
        in_specs=[
            pl.BlockSpec((tm, k_dim), lambda m: (m + m0, 0)),
            pl.BlockSpec((k_dim, d), lambda m: (0, 0), pipeline_mode=pl.Buffered(1)),
            pl.BlockSpec((tm, d), lambda m: (m + m0, 0)),
            pl.BlockSpec((1, d), lambda m: (0, 0)),
            pl.BlockSpec((1, d), lambda m: (0, 0)),
        ],
        out_specs=[pl.BlockSpec((tm, d), lambda m: (m, 0)) for _ in out_dtypes],
        out_shape=[jax.ShapeDtypeStruct((n_rows, d), dt) for dt in out_dtypes],
        compiler_params=_params(),
        name="proj_res_ln",
    )(a16, w16, res32, gain.reshape(1, d), bias.reshape(1, d))


def _conv_taps(wconv_ref):
    wc = wconv_ref[...]
    return wc[0:1], wc[1:2], wc[2:3]


def _gates(x_ref, w16_refs):
    x = x_ref[...]
    b_gate, c_gate, h = (jnp.dot(x, w[...], preferred_element_type=F32) for w in w16_refs)
    return b_gate, c_gate * h


def _mixer_in_long_kernel(x_ref, wb_ref, wc_ref, wh_ref, wconv_ref, g_ref, st_ref,
                          wb16_ref, wc16_ref, wh16_ref, hist_ref, *, tiles_per_seq, tm):
    m = pl.program_id(1)
    w16 = (wb16_ref, wc16_ref, wh16_ref)
    _cast_once(m, zip((wb_ref, wc_ref, wh_ref), w16))
    b_gate, u = _gates(x_ref, w16)

    @pl.when(m % tiles_per_seq == 0)
    def _():
        hist_ref[0:SUBLANES, :] = jnp.zeros((SUBLANES, hist_ref.shape[1]), F32)

    hist_ref[SUBLANES:SUBLANES + tm, :] = u
    w0, w1, w2 = _conv_taps(wconv_ref)
    conv = (w0 * hist_ref[SUBLANES - 2:SUBLANES - 2 + tm, :]
            + w1 * hist_ref[SUBLANES - 1:SUBLANES - 1 + tm, :]
            + w2 * u)
    g_ref[...] = (b_gate * conv).astype(g_ref.dtype)
    st_ref[0] = u[tm - (CONV_W - 1):tm, :]
    hist_ref[0:SUBLANES, :] = u[tm - SUBLANES:tm, :]


def _mixer_w_specs(d, tn, n_blocks, index):
    return [pl.BlockSpec((None, d, tn), index(j * n_blocks)) for j in range(3)]


def mixer_in_long(x16, w_a_in, w_a_conv, layer, n_seq, seq_len):
    d = x16.shape[1]
    tm = _pick_tile(seq_len, (512, 256, 128))
    tn = _pick_tile(d, (512, 256, 128))
    n_blocks = d // tn
    tiles_per_seq = seq_len // tm
    kern = functools.partial(_mixer_in_long_kernel, tiles_per_seq=tiles_per_seq, tm=tm)
    return pl.pallas_call(
        kern,
        grid=(n_blocks, n_seq * tiles_per_seq),
        in_specs=[pl.BlockSpec((tm, d), lambda n, m: (m, 0))]
        + _mixer_w_specs(d, tn, n_blocks, lambda off: (lambda n, m: (layer, 0, n + off)))
        + [pl.BlockSpec((None, CONV_W, tn), lambda n, m: (layer, 0, n))],
        out_specs=[
            pl.BlockSpec((tm, tn), lambda n, m: (m, n)),
            pl.BlockSpec((1, CONV_W - 1, tn), lambda n, m: (m // tiles_per_seq, 0, n)),
        ],
        out_shape=[jax.ShapeDtypeStruct((n_seq * seq_len, d), BF16),
                   jax.ShapeDtypeStruct((n_seq, CONV_W - 1, d), F32)],
        scratch_shapes=[pltpu.VMEM((d, tn), BF16)] * 3 + [pltpu.VMEM((tm + SUBLANES, tn), F32)],
        compiler_params=_params(),
        name="mixer_in_long",
    )(x16, w_a_in, w_a_in, w_a_in, w_a_conv)


def _mixer_in_short_kernel(x_ref, wb_ref, wc_ref, wh_ref, wconv_ref, prev_ref, g_ref, st_ref,
                           *, seq_len):
    rows, tn = x_ref.shape[0], wb_ref.shape[1]
    n_seq = rows // seq_len
    x = x_ref[...]
    b_gate, c_gate, h = (jnp.dot(x, w[...].astype(BF16), preferred_element_type=F32)
                         for w in (wb_ref, wc_ref, wh_ref))
    u3 = (c_gate * h).reshape(n_seq, seq_len, tn)
    prev = prev_ref[...]
    p0, p1 = prev[:, 0:1, :], prev[:, 1:2, :]
    pos = lax.broadcasted_iota(jnp.int32, u3.shape, 1)
    u1 = jnp.where(pos == 0, p1, pltpu.roll(u3, 1, axis=1))
    u2 = jnp.where(pos == 0, p0, jnp.where(pos == 1, p1, pltpu.roll(u3, 2, axis=1)))
    w0, w1, w2 = _conv_taps(wconv_ref)
    conv = w0 * u2 + w1 * u1 + w2 * u3
    g_ref[...] = (b_gate * conv.reshape(rows, tn)).astype(g_ref.dtype)
    st_ref[...] = u3[:, seq_len - (CONV_W - 1):seq_len, :]


def mixer_in_short(x16, w_a_in, w_a_conv, prev, layer, row_start, n_seq, seq_len):
    d = x16.shape[1]
    rows = n_seq * seq_len
    assert seq_len == SUBLANES and row_start % rows == 0
    tn = _pick_tile(d, (512, 256, 128))
    n_blocks = d // tn
    kern = functools.partial(_mixer_in_short_kernel, seq_len=seq_len)
    return pl.pallas_call(
        kern,
        grid=(n_blocks,),
        in_specs=[pl.BlockSpec((rows, d), lambda n: (row_start // rows, 0))]
        + _mixer_w_specs(d, tn, n_blocks, lambda off: (lambda n: (layer, 0, n + off)))
        + [pl.BlockSpec((None, CONV_W, tn), lambda n: (layer, 0, n)),
           pl.BlockSpec((None, n_seq, CONV_W - 1, tn), lambda n: (layer, 0, 0, n))],
        out_specs=[
            pl.BlockSpec((rows, tn), lambda n: (0, n)),
            pl.BlockSpec((n_seq, CONV_W - 1, tn), lambda n: (0, 0, n)),
        ],
        out_shape=[jax.ShapeDtypeStruct((rows, d), BF16),
                   jax.ShapeDtypeStruct((n_seq, CONV_W - 1, d), F32)],
        compiler_params=_params(),
        name="mixer_in_short",
    )(x16, w_a_in, w_a_in, w_a_in, w_a_conv, prev)


def _kv_proj_kernel(x_ref, wk_ref, wv_ref, k32_ref, v32_ref, k16_ref, v16_ref, *, head_dim):
    x = x_ref[...]
    tm, heads = k32_ref.shape[:2]
    for w_ref, o32_ref, o16_ref in ((wk_ref, k32_ref, k16_ref), (wv_ref, v32_ref, v16_ref)):
        y = jnp.dot(x, w_ref[...], preferred_element_type=F32)
        o16_ref[...] = y.astype(BF16)
        rows = o32_ref.reshape(tm * heads, head_dim)
        for h in range(heads):
            rows[pl.ds(h, tm, stride=heads), :] = y[:, h * head_dim:(h + 1) * head_dim]


def kv_proj(x16, w_kv16, row_start, n_rows, n_heads, head_dim):
    d = x16.shape[1]
    tm = _pick_tile(math.gcd(n_rows, row_start) if row_start else n_rows, (512, 256, 128))
    heads_per_block = SUBLANES
    tn = heads_per_block * head_dim
    n_blocks = d // tn
    m0 = row_start // tm
    spec32 = pl.BlockSpec((tm, heads_per_block, head_dim), lambda n, m: (m, n, 0))
    spec16 = pl.BlockSpec((tm, tn), lambda n, m: (m, n))
    return pl.pallas_call(
        functools.partial(_kv_proj_kernel, head_dim=head_dim),
        grid=(n_blocks, n_rows // tm),
        in_specs=[
            pl.BlockSpec((tm, d), lambda n, m: (m + m0, 0)),
            pl.BlockSpec((d, tn), lambda n, m: (0, n)),
            pl.BlockSpec((d, tn), lambda n, m: (0, n + n_blocks)),
        ],
        out_specs=[spec32, spec32, spec16, spec16],
        out_shape=[jax.ShapeDtypeStruct((n_rows, n_heads, head_dim), F32)] * 2
                  + [jax.ShapeDtypeStruct((n_rows, d), BF16)] * 2,
        compiler_params=_params(),
        name="kv_proj",
    )(x16, w_kv16, w_kv16)


def _q_proj_kernel(x_ref, w_ref, o_ref, w16_ref, *, scale):
    _cast_once(pl.program_id(1), ((w_ref, w16_ref),))
    q = jnp.dot(x_ref[...], w16_ref[...], preferred_element_type=F32)
    o_ref[...] = (q * scale).astype(o_ref.dtype)


def q_proj(x16, w_q, layer, scale):
    m_rows, d = x16.shape
    tm = _pick_tile(m_rows, (1056, 1024, 512, 256, 128))
    tn = _pick_tile(d, (512, 256, 128))
    return pl.pallas_call(
        functools.partial(_q_proj_kernel, scale=scale),
        grid=(d // tn, m_rows // tm),
        in_specs=[
            pl.BlockSpec((tm, d), lambda n, m: (m, 0)),
            pl.BlockSpec((None, d, tn), lambda n, m: (layer, 0, n)),
        ],
        out_specs=pl.BlockSpec((tm, tn), lambda n, m: (m, n)),
        out_shape=jax.ShapeDtypeStruct((m_rows, d), BF16),
        scratch_shapes=[pltpu.VMEM((d, tn), BF16)],
        compiler_params=_params(),
        name="q_proj",
    )(x16, w_q)


def _suffix_matrix(n):
    j = lax.broadcasted_iota(jnp.int32, (2 * n, n), 0) % n
    s = lax.broadcasted_iota(jnp.int32, (2 * n, n), 1)
    return (j > s).astype(BF16)


def _log2_keep(zn, causal):
    lk = jnp.minimum(zn, 0.0) - jnp.log2(1.0 + jnp.exp2(-jnp.abs(zn)))
    return lk if causal is None else jnp.where(causal, lk, 0.0)


def _stick_weights(zn, lk, suffix, carry):
    hi = lk.astype(BF16)
    lo = (lk - hi.astype(F32)).astype(BF16)
    after = jnp.dot(jnp.concatenate([hi, lo], axis=1), suffix, preferred_element_type=F32)
    w = jnp.exp2((lk - zn) + after + carry)
    return w, carry + jnp.sum(lk, axis=-1, keepdims=True)


_NT = (((1,), (1,)), ((), ()))


def _sb_prompt_kernel(q_ref, k_ref, v_ref, suffix_ref, o_ref, *, tq, sub, n_sub):
    qi = pl.program_id(2)
    q = q_ref[...]
    suffix = suffix_ref[...]
    head_dim = q.shape[1]
    step = sub * n_sub

    def visit(key_start, carry, acc, causal):
        k = k_ref[pl.ds(key_start, step), :]
        v = v_ref[pl.ds(key_start, step), :]
        zn = lax.dot_general(q, k, _NT, preferred_element_type=F32)
        lk = _log2_keep(zn, causal)
        ws = [None] * n_sub
        for s in reversed(range(n_sub)):
            cols = slice(s * sub, (s + 1) * sub)
            ws[s], carry = _stick_weights(zn[:, cols], lk[:, cols], suffix, carry)
        w = jnp.concatenate(ws, axis=1)
        if causal is not None:
            w = jnp.where(causal, w, 0.0)
        return carry, acc + jnp.dot(w.astype(BF16), v, preferred_element_type=F32)

    carry = jnp.zeros((tq, 1), F32)
    acc = jnp.zeros((tq, head_dim), F32)
    t = lax.broadcasted_iota(jnp.int32, (tq, step), 0)
    s = lax.broadcasted_iota(jnp.int32, (tq, step), 1)
    n_diag = tq // step
    for d in reversed(range(n_diag)):
        carry, acc = visit(pl.multiple_of(qi * tq + d * step, step), carry, acc, s + d * step < t)

    def body(i, state):
        return visit(pl.multiple_of((qi * n_diag - 1 - i) * step, step), state[0], state[1], None)

    carry, acc = lax.fori_loop(0, qi * n_diag, body, (carry, acc))
    o_ref[...] = acc.astype(o_ref.dtype)


def sb_attention_prompt(q16, k16, v16, n_seq, seq_len, n_heads, head_dim):
    tq = _pick_tile(seq_len, (512, 256, 128))
    sub = min(256, tq)
    n_sub = tq // sub
    n_q = seq_len // tq
    kern = functools.partial(_sb_prompt_kernel, tq=tq, sub=sub, n_sub=n_sub)
    return pl.pallas_call(
        kern,
        grid=(n_seq, n_heads, n_q),
        in_specs=[
            pl.BlockSpec((tq, head_dim), lambda b, h, i: (b * n_q + i, h)),
            pl.BlockSpec((seq_len, head_dim), lambda b, h, i: (b, h)),
            pl.BlockSpec((seq_len, head_dim), lambda b, h, i: (b, h)),
            pl.BlockSpec((2 * sub, sub), lambda b, h, i: (0, 0)),
        ],
        out_specs=pl.BlockSpec((tq, head_dim), lambda b, h, i: (b * n_q + i, h)),
        out_shape=jax.ShapeDtypeStruct(k16.shape, BF16),
        compiler_params=_params(),
        name="sb_attention_prompt",
    )(q16, k16, v16, _suffix_matrix(sub))


def _sb_sample_kernel(pt_ref, q_ref, knew_ref, vnew_ref, ck_ref, cv_ref, suffix_ref, o_ref,
                      kbuf, vbuf, sem, acc_ref, carry_ref, *, pages_per_step, n_heads, dec_seq):
    c = pl.program_id(1)
    n_steps = pl.num_programs(1)
    t = pl.program_id(0) * n_steps + c
    last = pl.num_programs(0) * n_steps - 1
    slot = t % 2
    head_dim = q_ref.shape[2]
    half_heads = n_heads // 2
    page = kbuf.shape[3]
    rows_per_pair = 2 * dec_seq
    suffix = suffix_ref[...]

    def page_copies(step, dst_slot):
        seq, grp = step // n_steps, step % n_steps
        copies = []
        for i in range(pages_per_step):
            pid = pt_ref[seq, (n_steps - 1 - grp) * pages_per_step + (pages_per_step - 1 - i)]
            keys = pl.ds(pl.multiple_of(pid * page, page), page)
            for half in range(2):
                heads = pl.ds(half * half_heads, half_heads)
                for src, dst in ((ck_ref, kbuf), (cv_ref, vbuf)):
                    copies.append(pltpu.make_async_copy(
                        src.at[keys, heads, :], dst.at[dst_slot, i, half], sem.at[dst_slot]))
        return copies

    @pl.when(t == 0)
    def _():
        for cp in page_copies(t, slot):
            cp.start()

    @pl.when(t < last)
    def _():
        for cp in page_copies(t + 1, 1 - slot):
            cp.start()

    def visit(k_head, v_head, causal):
        zs = []
        for hp in range(n_heads // 2):
            lhs = q_ref[0, hp * rows_per_pair:(hp + 1) * rows_per_pair, :]
            for half in range(2):
                z_h = lax.dot_general(lhs, k_head(2 * hp + half), _NT, preferred_element_type=F32)
                zs.append(z_h[half * dec_seq:(half + 1) * dec_seq])
        zn = jnp.concatenate(zs, axis=0)
        w, carry = _stick_weights(zn, _log2_keep(zn, causal), suffix, carry_ref[...])
        if causal is not None:
            w = jnp.where(causal, w, 0.0)
        carry_ref[...] = carry
        w16 = w.astype(BF16)
        outs = []
        for hp in range(n_heads // 2):
            lhs = w16[hp * rows_per_pair:(hp + 1) * rows_per_pair, :]
            for half in range(2):
                o_h = jnp.dot(lhs, v_head(2 * hp + half), preferred_element_type=F32)
                outs.append(o_h[half * dec_seq:(half + 1) * dec_seq])
        acc_ref[...] += jnp.concatenate(outs, axis=0)

    def lane_slab(ref):
        return lambda h: ref[0, :, h * head_dim:(h + 1) * head_dim]

    def strided_rows(buf, i):
        def get(h):
            rows = buf.at[slot, i, h // half_heads].reshape(page * half_heads, head_dim)
            return rows[pl.ds(h % half_heads, page, stride=half_heads), :].astype(BF16)
        return get

    @pl.when(c == 0)
    def _():
        acc_ref[...] = jnp.zeros_like(acc_ref)
        carry_ref[...] = jnp.zeros_like(carry_ref)
        n_new = knew_ref.shape[1]
        row = lax.broadcasted_iota(jnp.int32, (n_heads * dec_seq, n_new), 0)
        col = lax.broadcasted_iota(jnp.int32, (n_heads * dec_seq, n_new), 1)
        visit(lane_slab(knew_ref), lane_slab(vnew_ref), col < (row % dec_seq))

    for cp in page_copies(t, slot):
        cp.wait()

    for i in range(pages_per_step):
        visit(strided_rows(kbuf, i), strided_rows(vbuf, i), None)

    @pl.when(c == n_steps - 1)
    def _():
        o_ref[0] = acc_ref[...].astype(o_ref.dtype)


def sb_attention_sample(q_hm, knew_pad, vnew_pad, cache_k, cache_v, page_table, dec_seq):
    n_seq, rows, head_dim = q_hm.shape
    n_phys, page, n_heads, _ = cache_k.shape
    n_pages = page_table.shape[1]
    pages_per_step = _pick_tile(n_pages, (4, 2, 1))
    n_steps = n_pages // pages_per_step
    assert n_heads % (2 * SUBLANES) == 0
    half_heads = n_heads // 2
    kern = functools.partial(_sb_sample_kernel, pages_per_step=pages_per_step,
                             n_heads=n_heads, dec_seq=dec_seq)
    page_buffers = pltpu.VMEM((2, pages_per_step, 2, page, half_heads, head_dim), F32)
    grid_spec = pltpu.PrefetchScalarGridSpec(
        num_scalar_prefetch=1,
        grid=(n_seq, n_steps),
        in_specs=[
            pl.BlockSpec((1, rows, head_dim), lambda b, c, pt: (b, 0, 0)),
            pl.BlockSpec((1, page, n_heads * head_dim), lambda b, c, pt: (b, 0, 0)),
            pl.BlockSpec((1, page, n_heads * head_dim), lambda b, c, pt: (b, 0, 0)),
            pl.BlockSpec(memory_space=pl.ANY),
            pl.BlockSpec(memory_space=pl.ANY),
            pl.BlockSpec((2 * page, page), lambda b, c, pt: (0, 0)),
        ],
        out_specs=pl.BlockSpec((1, rows, head_dim), lambda b, c, pt: (b, 0, 0)),
        scratch_shapes=[page_buffers, page_buffers, pltpu.SemaphoreType.DMA((2,)),
                        pltpu.VMEM((rows, head_dim), F32), pltpu.VMEM((rows, 1), F32)],
    )
    ck = cache_k.reshape(n_phys * page, n_heads, head_dim)
    cv = cache_v.reshape(n_phys * page, n_heads, head_dim)
    return pl.pallas_call(
        kern,
        grid_spec=grid_spec,
        out_shape=jax.ShapeDtypeStruct((n_seq, rows, head_dim), BF16),
        compiler_params=pltpu.CompilerParams(vmem_limit_bytes=VMEM_LIMIT_BYTES,
                                             dimension_semantics=("arbitrary", "arbitrary")),
        name="sb_attention_sample",
    )(page_table, q_hm, knew_pad, vnew_pad, ck, cv, _suffix_matrix(page))


def kernel(x_prompt, x_sample, state_conv, cache_k, cache_v, page_table, ln_g, ln_b, w_ffn_in,
           w_ffn_out, w_a_in, w_a_conv, w_a_out, w_kv, w_q, w_o):
    n_p, seq, d = x_prompt.shape
    n_s, dec_seq, _ = x_sample.shape
    depth = ln_g.shape[0]
    page, n_heads, head_dim = cache_k.shape[1:]
    rows_p, rows_s = n_p * seq, n_s * dec_seq
    alpha = (2.0 * depth) ** 0.25
    assert w_a_in.shape[0] == 1 and depth == 2, "one short-conv layer followed by one attention layer"

    x32 = jnp.concatenate([x_prompt.reshape(rows_p, d), x_sample.reshape(rows_s, d)], axis=0)
    x16 = x32.astype(BF16)

    def ffn(x32, x16, layer, half, ln_idx, row_splits=None):
        h = swiglu_in(x16, w_ffn_in, layer, half)
        w_out16 = cast_weight(w_ffn_out, (layer, half))
        norm = functools.partial(proj_res_ln, h, w_out16, x32, ln_g[layer, ln_idx], ln_b[layer, ln_idx],
                                 alpha=alpha, coef=0.5)
        if row_splits is None:
            return norm()
        return [norm(row_start=r0, n_rows=n, out_dtypes=(F32,))[0] for r0, n in row_splits]

    x32, x16 = ffn(x32, x16, 0, 0, 0)
    g, conv_prompt = mixer_in_long(x16, w_a_in, w_a_conv, 0, n_p, seq)
    g_s, conv_sample = mixer_in_short(x16, w_a_in, w_a_conv, state_conv, 0, rows_p, n_s, dec_seq)
    x32, x16 = proj_res_ln(jnp.concatenate([g, g_s], axis=0), cast_weight(w_a_out, (0,)), x32, ln_g[0, 1], ln_b[0, 1],
                           alpha=alpha, coef=1.0)
    x32, x16 = ffn(x32, x16, 0, 1, 2)

    w_kv16 = w_kv.astype(BF16)
    k_prompt, v_prompt, k16_p, v16_p = kv_proj(x16, w_kv16, 0, rows_p, n_heads, head_dim)
    k_sample, v_sample, k16_s, v16_s = kv_proj(x16, w_kv16, rows_p, rows_s, n_heads, head_dim)

    x32, x16 = ffn(x32, x16, 1, 0, 0)
    q16 = q_proj(x16, w_q, 0, -math.log2(math.e) * head_dim ** -0.5)
    o = sb_attention_prompt(q16, k16_p, v16_p, n_p, seq, n_heads, head_dim)

    q_hm = (q16[rows_p:].reshape(n_s, dec_seq, n_heads, head_dim)
            .transpose(0, 2, 1, 3).reshape(n_s, n_heads * dec_seq, head_dim))
    pad = ((0, 0), (0, page - dec_seq), (0, 0))
    o_hm = sb_attention_sample(q_hm, jnp.pad(k16_s.reshape(n_s, dec_seq, d), pad),
                               jnp.pad(v16_s.reshape(n_s, dec_seq, d), pad),
                               cache_k, cache_v, page_table, dec_seq)
    o_s = o_hm.reshape(n_s, n_heads, dec_seq, head_dim).transpose(0, 2, 1, 3).reshape(rows_s, d)

    x32, x16 = proj_res_ln(jnp.concatenate([o, o_s], axis=0), cast_weight(w_o, (0,)), x32, ln_g[1, 1], ln_b[1, 1],
                           alpha=alpha, coef=1.0)
    y_prompt, y_sample = ffn(x32, x16, 1, 1, 2, row_splits=((0, rows_p), (rows_p, rows_s)))

    return (y_prompt.reshape(n_p, seq, d), y_sample.reshape(n_s, dec_seq, d),
            conv_prompt[None], conv_sample[None],
            k_prompt.reshape(n_p, seq, n_heads, head_dim), v_prompt.reshape(n_p, seq, n_heads, head_dim),
            k_sample.reshape(n_s, dec_seq, n_heads, head_dim),
            v_sample.reshape(n_s, dec_seq, n_heads, head_dim))
```

```python
import functools
import math

import jax
import jax.numpy as jnp
from jax import lax
from jax.experimental import pallas as pl
from jax.experimental.pallas import tpu as pltpu

F32 = jnp.float32
BF16 = jnp.bfloat16

LN_EPS = 1e-5
CONV_W = 3
SUBLANES = 8
VMEM_LIMIT_BYTES = 56 * 1024 * 1024
ALL_WEIGHTS_ZERO_LOG2 = -160.0


def _params():
    return pltpu.CompilerParams(vmem_limit_bytes=VMEM_LIMIT_BYTES)


def _pick_tile(total, candidates):
    for c in candidates:
        if total % c == 0:
            return c
    return total


def _cast_once(step, pairs):
    @pl.when(step == 0)
    def _():
        for src, dst in pairs:
            dst[...] = src[...].astype(BF16)


def _cast_kernel(w_ref, o_ref):
    o_ref[...] = w_ref[...].astype(BF16)


def cast_weight(w_stack, lead):
    k_dim, n_dim = w_stack.shape[-2:]
    tk = _pick_tile(k_dim, (512, 256, 128))
    nones = (None,) * len(lead)
    return pl.pallas_call(
        _cast_kernel,
        grid=(k_dim // tk,),
        in_specs=[pl.BlockSpec(nones + (tk, n_dim), lambda k: lead + (k, 0))],
        out_specs=pl.BlockSpec((tk, n_dim), lambda k: (k, 0)),
        out_shape=jax.ShapeDtypeStruct((k_dim, n_dim), BF16),
        compiler_params=_params(),
        name="cast_weight",
    )(w_stack)


def _swiglu_in_kernel(x_ref, wa_ref, wg_ref, wout_ref, o_ref, wout16_ref, wa16_ref, wg16_ref):
    _cast_once(pl.program_id(1), ((wa_ref, wa16_ref), (wg_ref, wg16_ref), (wout_ref, wout16_ref)))
    x = x_ref[...]
    a = jnp.dot(x, wa16_ref[...], preferred_element_type=F32)
    g = jnp.dot(x, wg16_ref[...], preferred_element_type=F32)
    o_ref[...] = (a * jax.nn.sigmoid(a) * g).astype(o_ref.dtype)


def swiglu_in(x16, w_ffn_in, w_ffn_out, layer, half):
    m_rows, d = x16.shape
    f = w_ffn_in.shape[-1] // 2
    tm = _pick_tile(m_rows, (1056, 1024, 512, 256, 128))
    tn = _pick_tile(f, (512, 256, 128))
    n_blocks = f // tn
    return pl.pallas_call(
        _swiglu_in_kernel,
        grid=(n_blocks, m_rows // tm),
        in_specs=[
            pl.BlockSpec((tm, d), lambda n, m: (m, 0)),
            pl.BlockSpec((None, None, d, tn), lambda n, m: (layer, half, 0, n)),
            pl.BlockSpec((None, None, d, tn), lambda n, m: (layer, half, 0, n + n_blocks)),
            pl.BlockSpec((None, None, tn, d), lambda n, m: (layer, half, n, 0)),
        ],
        out_specs=[pl.BlockSpec((tm, tn), lambda n, m: (m, n)),
                   pl.BlockSpec((tn, d), lambda n, m: (n, 0))],
        out_shape=[jax.ShapeDtypeStruct((m_rows, f), BF16), jax.ShapeDtypeStruct((f, d), BF16)],
        scratch_shapes=[pltpu.VMEM((d, tn), BF16)] * 2,
        compiler_params=_params(),
        name="swiglu_in",
    )(x16, w_ffn_in, w_ffn_in, w_ffn_out)


def _proj_res_ln_kernel(a_ref, w_ref, res_ref, gain_ref, bias_ref, *out_refs, alpha, coef, chunk):
    for r in range(a_ref.shape[0] // chunk):
        rows = slice(r * chunk, (r + 1) * chunk)
        y = alpha * res_ref[rows, :] + coef * jnp.dot(a_ref[rows, :], w_ref[...],
                                                      preferred_element_type=F32)
        mu = jnp.mean(y, axis=-1, keepdims=True)
        dev = y - mu
        var = jnp.mean(dev * dev, axis=-1, keepdims=True)
        out = dev * lax.rsqrt(var + LN_EPS) * gain_ref[...] + bias_ref[...]
        for o_ref in out_refs:
            o_ref[rows, :] = out.astype(o_ref.dtype)


def proj_res_ln(a16, w16, res32, gain, bias, *, alpha, coef, row_start=0, n_rows=None,
                out_dtypes=(F32, BF16)):
    k_dim = a16.shape[1]
    d = w16.shape[1]
    n_rows = a16.shape[0] if n_rows is None else n_rows
    tiles = (528, 512, 256, 128) if k_dim <= d else (384, 256, 128)
    tm = _pick_tile(math.gcd(n_rows, row_start) if row_start else n_rows, tiles)
    chunk = _pick_tile(tm, (192, 176, 128))
    m0 = row_start // tm
    kern = functools.partial(_proj_res_ln_kernel, alpha=alpha, coef=coef, chunk=chunk)
    return pl.pallas_call(
        kern,
        grid=(n_rows // tm,),
        in_specs=[
            pl.BlockSpec((tm, k_dim), lambda m: (m + m0, 0)),
            pl.BlockSpec((k_dim, d), lambda m: (0, 0), pipeline_mode=pl.Buffered(1)),
            pl.BlockSpec((tm, d), lambda m: (m + m0, 0)),
            pl.BlockSpec((1, d), lambda m: (0, 0)),
            pl.BlockSpec((1, d), lambda m: (0, 0)),
        ],
        out_specs=[pl.BlockSpec((tm, d), lambda m: (m, 0)) for _ in out_dtypes],
        out_shape=[jax.ShapeDtypeStruct((n_rows, d), dt) for dt in out_dtypes],
        compiler_params=_params(),
        name="proj_res_ln",
    )(a16, w16, res32, gain.reshape(1, d), bias.reshape(1, d))


def _conv_taps(wconv_ref):
    wc = wconv_ref[...]
    return wc[0:1], wc[1:2], wc[2:3]


def _gates(x_ref, w16_refs):
    x = x_ref[...]
    b_gate, c_gate, h = (jnp.dot(x, w[...], preferred_element_type=F32) for w in w16_refs)
    return b_gate, c_gate * h


def _mixer_in_long_kernel(x_ref, wb_ref, wc_ref, wh_ref, wconv_ref, g_ref, st_ref,
                          wb16_ref, wc16_ref, wh16_ref, hist_ref, *, tiles_per_seq, tm):
    m = pl.program_id(1)
    w16 = (wb16_ref, wc16_ref, wh16_ref)
    _cast_once(m, zip((wb_ref, wc_ref, wh_ref), w16))
    b_gate, u = _gates(x_ref, w16)

    @pl.when(m % tiles_per_seq == 0)
    def _():
        hist_ref[0:SUBLANES, :] = jnp.zeros((SUBLANES, hist_ref.shape[1]), F32)

    hist_ref[SUBLANES:SUBLANES + tm, :] = u
    w0, w1, w2 = _conv_taps(wconv_ref)
    conv = (w0 * hist_ref[SUBLANES - 2:SUBLANES - 2 + tm, :]
            + w1 * hist_ref[SUBLANES - 1:SUBLANES - 1 + tm, :]
            + w2 * u)
    g_ref[...] = (b_gate * conv).astype(g_ref.dtype)
    st_ref[0] = u[tm - (CONV_W - 1):tm, :]
    hist_ref[0:SUBLANES, :] = u[tm - SUBLANES:tm, :]


def _mixer_w_specs(d, tn, n_blocks, index):
    return [pl.BlockSpec((None, d, tn), index(j * n_blocks)) for j in range(3)]


def mixer_in_long(x16, w_a_in, w_a_conv, layer, n_seq, seq_len):
    d = x16.shape[1]
    tm = _pick_tile(seq_len, (512, 256, 128))
    tn = _pick_tile(d, (512, 256, 128))
    n_blocks = d // tn
    tiles_per_seq = seq_len // tm
    kern = functools.partial(_mixer_in_long_kernel, tiles_per_seq=tiles_per_seq, tm=tm)
    return pl.pallas_call(
        kern,
        grid=(n_blocks, n_seq * tiles_per_seq),
        in_specs=[pl.BlockSpec((tm, d), lambda n, m: (m, 0))]
        + _mixer_w_specs(d, tn, n_blocks, lambda off: (lambda n, m: (layer, 0, n + off)))
        + [pl.BlockSpec((None, CONV_W, tn), lambda n, m: (layer, 0, n))],
        out_specs=[
            pl.BlockSpec((tm, tn), lambda n, m: (m, n)),
            pl.BlockSpec((1, CONV_W - 1, tn), lambda n, m: (m // tiles_per_seq, 0, n)),
        ],
        out_shape=[jax.ShapeDtypeStruct((n_seq * seq_len, d), BF16),
                   jax.ShapeDtypeStruct((n_seq, CONV_W - 1, d), F32)],
        scratch_shapes=[pltpu.VMEM((d, tn), BF16)] * 3 + [pltpu.VMEM((tm + SUBLANES, tn), F32)],
        compiler_params=_params(),
        name="mixer_in_long",
    )(x16, w_a_in, w_a_in, w_a_in, w_a_conv)


def _mixer_in_short_kernel(x_ref, wb_ref, wc_ref, wh_ref, wconv_ref, prev_ref, g_ref, st_ref,
                           *, seq_len):
    rows, tn = x_ref.shape[0], wb_ref.shape[1]
    n_seq = rows // seq_len
    x = x_ref[...]
    b_gate, c_gate, h = (jnp.dot(x, w[...].astype(BF16), preferred_element_type=F32)
                         for w in (wb_ref, wc_ref, wh_ref))
    u3 = (c_gate * h).reshape(n_seq, seq_len, tn)
    prev = prev_ref[...]
    p0, p1 = prev[:, 0:1, :], prev[:, 1:2, :]
    pos = lax.broadcasted_iota(jnp.int32, u3.shape, 1)
    u1 = jnp.where(pos == 0, p1, pltpu.roll(u3, 1, axis=1))
    u2 = jnp.where(pos == 0, p0, jnp.where(pos == 1, p1, pltpu.roll(u3, 2, axis=1)))
    w0, w1, w2 = _conv_taps(wconv_ref)
    conv = w0 * u2 + w1 * u1 + w2 * u3
    g_ref[...] = (b_gate * conv.reshape(rows, tn)).astype(g_ref.dtype)
    st_ref[...] = u3[:, seq_len - (CONV_W - 1):seq_len, :]


def mixer_in_short(x16, w_a_in, w_a_conv, prev, layer, row_start, n_seq, seq_len):
    d = x16.shape[1]
    rows = n_seq * seq_len
    assert seq_len == SUBLANES and row_start % rows == 0
    tn = _pick_tile(d, (512, 256, 128))
    n_blocks = d // tn
    kern = functools.partial(_mixer_in_short_kernel, seq_len=seq_len)
    return pl.pallas_call(
        kern,
        grid=(n_blocks,),
        in_specs=[pl.BlockSpec((rows, d), lambda n: (row_start // rows, 0))]
        + _mixer_w_specs(d, tn, n_blocks, lambda off: (lambda n: (layer, 0, n + off)))
        + [pl.BlockSpec((None, CONV_W, tn), lambda n: (layer, 0, n)),
           pl.BlockSpec((None, n_seq, CONV_W - 1, tn), lambda n: (layer, 0, 0, n))],
        out_specs=[
            pl.BlockSpec((rows, tn), lambda n: (0, n)),
            pl.BlockSpec((n_seq, CONV_W - 1, tn), lambda n: (0, 0, n)),
        ],
        out_shape=[jax.ShapeDtypeStruct((rows, d), BF16),
                   jax.ShapeDtypeStruct((n_seq, CONV_W - 1, d), F32)],
        compiler_params=_params(),
        name="mixer_in_short",
    )(x16, w_a_in, w_a_in, w_a_in, w_a_conv, prev)


def _kv_proj_kernel(x_ref, wk_ref, wv_ref, k32_ref, v32_ref, k16_ref, v16_ref, *, head_dim):
    x = x_ref[...]
    tm, heads = k32_ref.shape[:2]
    for w_ref, o32_ref, o16_ref in ((wk_ref, k32_ref, k16_ref), (wv_ref, v32_ref, v16_ref)):
        y = jnp.dot(x, w_ref[...], preferred_element_type=F32)
        o16_ref[...] = y.astype(BF16)
        rows = o32_ref.reshape(tm * heads, head_dim)
        for h in range(heads):
            rows[pl.ds(h, tm, stride=heads), :] = y[:, h * head_dim:(h + 1) * head_dim]


def kv_proj(x16, w_kv16, row_start, n_rows, n_heads, head_dim):
    d = x16.shape[1]
    tm = _pick_tile(math.gcd(n_rows, row_start) if row_start else n_rows, (512, 256, 128))
    heads_per_block = SUBLANES
    tn = heads_per_block * head_dim
    n_blocks = d // tn
    m0 = row_start // tm
    spec32 = pl.BlockSpec((tm, heads_per_block, head_dim), lambda n, m: (m, n, 0))
    spec16 = pl.BlockSpec((tm, tn), lambda n, m: (m, n))
    return pl.pallas_call(
        functools.partial(_kv_proj_kernel, head_dim=head_dim),
        grid=(n_blocks, n_rows // tm),
        in_specs=[
            pl.BlockSpec((tm, d), lambda n, m: (m + m0, 0)),
            pl.BlockSpec((d, tn), lambda n, m: (0, n)),
            pl.BlockSpec((d, tn), lambda n, m: (0, n + n_blocks)),
        ],
        out_specs=[spec32, spec32, spec16, spec16],
        out_shape=[jax.ShapeDtypeStruct((n_rows, n_heads, head_dim), F32)] * 2
                  + [jax.ShapeDtypeStruct((n_rows, d), BF16)] * 2,
        compiler_params=_params(),
        name="kv_proj",
    )(x16, w_kv16, w_kv16)


def _q_proj_kernel(x_ref, w_ref, o_ref, w16_ref, *, scale):
    _cast_once(pl.program_id(1), ((w_ref, w16_ref),))
    q = jnp.dot(x_ref[...], w16_ref[...], preferred_element_type=F32)
    o_ref[...] = (q * scale).astype(o_ref.dtype)


def q_proj(x16, w_q, layer, scale):
    m_rows, d = x16.shape
    tm = _pick_tile(m_rows, (1056, 1024, 512, 256, 128))
    tn = _pick_tile(d, (512, 256, 128))
    return pl.pallas_call(
        functools.partial(_q_proj_kernel, scale=scale),
        grid=(d // tn, m_rows // tm),
        in_specs=[
            pl.BlockSpec((tm, d), lambda n, m: (m, 0)),
            pl.BlockSpec((None, d, tn), lambda n, m: (layer, 0, n)),
        ],
        out_specs=pl.BlockSpec((tm, tn), lambda n, m: (m, n)),
        out_shape=jax.ShapeDtypeStruct((m_rows, d), BF16),
        scratch_shapes=[pltpu.VMEM((d, tn), BF16)],
        compiler_params=_params(),
        name="q_proj",
    )(x16, w_q)


def _suffix_matrix(n):
    j = lax.broadcasted_iota(jnp.int32, (2 * n, n), 0) % n
    s = lax.broadcasted_iota(jnp.int32, (2 * n, n), 1)
    return (j > s).astype(BF16)


def _log2_keep(zn, causal):
    lk = jnp.minimum(zn, 0.0) - jnp.log2(1.0 + jnp.exp2(-jnp.abs(zn)))
    return lk if causal is None else jnp.where(causal, lk, 0.0)


def _stick_weights(zn, lk, suffix, carry):
    hi = lk.astype(BF16)
    lo = (lk - hi.astype(F32)).astype(BF16)
    after = jnp.dot(jnp.concatenate([hi, lo], axis=1), suffix, preferred_element_type=F32)
    w = jnp.exp2((lk - zn) + after + carry)
    return w, carry + jnp.sum(lk, axis=-1, keepdims=True)


_NT = (((1,), (1,)), ((), ()))


def _sb_prompt_kernel(q_ref, k_ref, v_ref, suffix_ref, o_ref, *, tq, sub, n_sub):
    qi = pl.program_id(2)
    q = q_ref[...]
    suffix = suffix_ref[...]
    head_dim = q.shape[1]
    step = sub * n_sub

    def visit(key_start, carry, acc, causal):
        k = k_ref[pl.ds(key_start, step), :]
        v = v_ref[pl.ds(key_start, step), :]
        zn = lax.dot_general(q, k, _NT, preferred_element_type=F32)
        lk = _log2_keep(zn, causal)
        ws = [None] * n_sub
        for s in reversed(range(n_sub)):
            cols = slice(s * sub, (s + 1) * sub)
            ws[s], carry = _stick_weights(zn[:, cols], lk[:, cols], suffix, carry)
        w = jnp.concatenate(ws, axis=1)
        if causal is not None:
            w = jnp.where(causal, w, 0.0)
        return carry, acc + jnp.dot(w.astype(BF16), v, preferred_element_type=F32)

    carry = jnp.zeros((tq, 1), F32)
    acc = jnp.zeros((tq, head_dim), F32)
    t = lax.broadcasted_iota(jnp.int32, (tq, step), 0)
    s = lax.broadcasted_iota(jnp.int32, (tq, step), 1)
    n_diag = tq // step
    for d in reversed(range(n_diag)):
        carry, acc = visit(pl.multiple_of(qi * tq + d * step, step), carry, acc, s + d * step < t)

    n_older = qi * n_diag

    def any_alive(carry):
        return jnp.max(carry) > ALL_WEIGHTS_ZERO_LOG2

    def cond(loop):
        i, alive, _, _ = loop
        return jnp.logical_and(i < n_older, alive)

    def body(loop):
        i, _, carry, acc = loop
        carry, acc = visit(pl.multiple_of((n_older - 1 - i) * step, step), carry, acc, None)
        return i + 1, any_alive(carry), carry, acc

    _, _, _, acc = lax.while_loop(cond, body, (0, any_alive(carry), carry, acc))
    o_ref[...] = acc.astype(o_ref.dtype)


def sb_attention_prompt(q16, k16, v16, n_seq, seq_len, n_heads, head_dim):
    tq = _pick_tile(seq_len, (512, 256, 128))
    sub = min(256, tq)
    n_sub = tq // sub
    n_q = seq_len // tq
    kern = functools.partial(_sb_prompt_kernel, tq=tq, sub=sub, n_sub=n_sub)
    return pl.pallas_call(
        kern,
        grid=(n_seq, n_heads, n_q),
        in_specs=[
            pl.BlockSpec((tq, head_dim), lambda b, h, i: (b * n_q + i, h)),
            pl.BlockSpec((seq_len, head_dim), lambda b, h, i: (b, h)),
            pl.BlockSpec((seq_len, head_dim), lambda b, h, i: (b, h)),
            pl.BlockSpec((2 * sub, sub), lambda b, h, i: (0, 0)),
        ],
        out_specs=pl.BlockSpec((tq, head_dim), lambda b, h, i: (b * n_q + i, h)),
        out_shape=jax.ShapeDtypeStruct(k16.shape, BF16),
        compiler_params=_params(),
        name="sb_attention_prompt",
    )(q16, k16, v16, _suffix_matrix(sub))


def _sb_sample_kernel(pt_ref, q_ref, knew_ref, vnew_ref, ck_ref, cv_ref, suffix_ref, o_ref,
                      kbuf, vbuf, sem, acc_ref, carry_ref, *, pages_per_step, n_heads, dec_seq):
    c = pl.program_id(1)
    n_steps = pl.num_programs(1)
    t = pl.program_id(0) * n_steps + c
    last = pl.num_programs(0) * n_steps - 1
    slot = t % 2
    head_dim = q_ref.shape[2]
    half_heads = n_heads // 2
    page = kbuf.shape[3]
    rows_per_pair = 2 * dec_seq
    suffix = suffix_ref[...]

    def page_copies(step, dst_slot):
        seq, grp = step // n_steps, step % n_steps
        copies = []
        for i in range(pages_per_step):
            pid = pt_ref[seq, (n_steps - 1 - grp) * pages_per_step + (pages_per_step - 1 - i)]
            keys = pl.ds(pl.multiple_of(pid * page, page), page)
            for half in range(2):
                heads = pl.ds(half * half_heads, half_heads)
                for src, dst in ((ck_ref, kbuf), (cv_ref, vbuf)):
                    copies.append(pltpu.make_async_copy(
                        src.at[keys, heads, :], dst.at[dst_slot, i, half], sem.at[dst_slot]))
        return copies

    @pl.when(t == 0)
    def _():
        for cp in page_copies(t, slot):
            cp.start()

    @pl.when(t < last)
    def _():
        for cp in page_copies(t + 1, 1 - slot):
            cp.start()

    def visit(k_head, v_head, causal):
        zs = []
        for hp in range(n_heads // 2):
            lhs = q_ref[0, hp * rows_per_pair:(hp + 1) * rows_per_pair, :]
            for half in range(2):
                z_h = lax.dot_general(lhs, k_head(2 * hp + half), _NT, preferred_element_type=F32)
                zs.append(z_h[half * dec_seq:(half + 1) * dec_seq])
        zn = jnp.concatenate(zs, axis=0)
        w, carry = _stick_weights(zn, _log2_keep(zn, causal), suffix, carry_ref[...])
        if causal is not None:
            w = jnp.where(causal, w, 0.0)
        carry_ref[...] = carry
        w16 = w.astype(BF16)
        outs = []
        for hp in range(n_heads // 2):
            lhs = w16[hp * rows_per_pair:(hp + 1) * rows_per_pair, :]
            for half in range(2):
                o_h = jnp.dot(lhs, v_head(2 * hp + half), preferred_element_type=F32)
                outs.append(o_h[half * dec_seq:(half + 1) * dec_seq])
        acc_ref[...] += jnp.concatenate(outs, axis=0)

    def lane_slab(ref):
        return lambda h: ref[0, :, h * head_dim:(h + 1) * head_dim]

    def strided_rows(buf, i):
        def get(h):
            rows = buf.at[slot, i, h // half_heads].reshape(page * half_heads, head_dim)
            return rows[pl.ds(h % half_heads, page, stride=half_heads), :].astype(BF16)
        return get

    @pl.when(c == 0)
    def _():
        acc_ref[...] = jnp.zeros_like(acc_ref)
        carry_ref[...] = jnp.zeros_like(carry_ref)
        n_new = knew_ref.shape[1]
        row = lax.broadcasted_iota(jnp.int32, (n_heads * dec_seq, n_new), 0)
        col = lax.broadcasted_iota(jnp.int32, (n_heads * dec_seq, n_new), 1)
        visit(lane_slab(knew_ref), lane_slab(vnew_ref), col < (row % dec_seq))

    for cp in page_copies(t, slot):
        cp.wait()

    for i in range(pages_per_step):
        visit(strided_rows(kbuf, i), strided_rows(vbuf, i), None)

    @pl.when(c == n_steps - 1)
    def _():
        o_ref[0] = acc_ref[...].astype(o_ref.dtype)


def sb_attention_sample(q_hm, knew_pad, vnew_pad, cache_k, cache_v, page_table, dec_seq):
    n_seq, rows, head_dim = q_hm.shape
    n_phys, page, n_heads, _ = cache_k.shape
    n_pages = page_table.shape[1]
    pages_per_step = _pick_tile(n_pages, (4, 2, 1))
    n_steps = n_pages // pages_per_step
    assert n_heads % (2 * SUBLANES) == 0
    half_heads = n_heads // 2
    kern = functools.partial(_sb_sample_kernel, pages_per_step=pages_per_step,
                             n_heads=n_heads, dec_seq=dec_seq)
    page_buffers = pltpu.VMEM((2, pages_per_step, 2, page, half_heads, head_dim), F32)
    grid_spec = pltpu.PrefetchScalarGridSpec(
        num_scalar_prefetch=1,
        grid=(n_seq, n_steps),
        in_specs=[
            pl.BlockSpec((1, rows, head_dim), lambda b, c, pt: (b, 0, 0)),
            pl.BlockSpec((1, page, n_heads * head_dim), lambda b, c, pt: (b, 0, 0)),
            pl.BlockSpec((1, page, n_heads * head_dim), lambda b, c, pt: (b, 0, 0)),
            pl.BlockSpec(memory_space=pl.ANY),
            pl.BlockSpec(memory_space=pl.ANY),
            pl.BlockSpec((2 * page, page), lambda b, c, pt: (0, 0)),
        ],
        out_specs=pl.BlockSpec((1, rows, head_dim), lambda b, c, pt: (b, 0, 0)),
        scratch_shapes=[page_buffers, page_buffers, pltpu.SemaphoreType.DMA((2,)),
                        pltpu.VMEM((rows, head_dim), F32), pltpu.VMEM((rows, 1), F32)],
    )
    ck = cache_k.reshape(n_phys * page, n_heads, head_dim)
    cv = cache_v.reshape(n_phys * page, n_heads, head_dim)
    return pl.pallas_call(
        kern,
        grid_spec=grid_spec,
        out_shape=jax.ShapeDtypeStruct((n_seq, rows, head_dim), BF16),
        compiler_params=pltpu.CompilerParams(vmem_limit_bytes=VMEM_LIMIT_BYTES,
                                             dimension_semantics=("arbitrary", "arbitrary")),
        name="sb_attention_sample",
    )(page_table, q_hm, knew_pad, vnew_pad, ck, cv, _suffix_matrix(page))


def kernel(x_prompt, x_sample, state_conv, cache_k, cache_v, page_table, ln_g, ln_b, w_ffn_in,
           w_ffn_out, w_a_in, w_a_conv, w_a_out, w_kv, w_q, w_o):
    n_p, seq, d = x_prompt.shape
    n_s, dec_seq, _ = x_sample.shape
    depth = ln_g.shape[0]
    page, n_heads, head_dim = cache_k.shape[1:]
    rows_p, rows_s = n_p * seq, n_s * dec_seq
    alpha = (2.0 * depth) ** 0.25
    assert w_a_in.shape[0] == 1 and depth == 2, "one short-conv layer followed by one attention layer"

    x32 = jnp.concatenate([x_prompt.reshape(rows_p, d), x_sample.reshape(rows_s, d)], axis=0)
    x16 = x32.astype(BF16)

    def ffn(x32, x16, layer, half, ln_idx, row_splits=None):
        h, w_out16 = swiglu_in(x16, w_ffn_in, w_ffn_out, layer, half)
        norm = functools.partial(proj_res_ln, h, w_out16, x32, ln_g[layer, ln_idx], ln_b[layer, ln_idx],
                                 alpha=alpha, coef=0.5)
        if row_splits is None:
            return norm()
        return [norm(row_start=r0, n_rows=n, out_dtypes=(F32,))[0] for r0, n in row_splits]

    x32, x16 = ffn(x32, x16, 0, 0, 0)
    g, conv_prompt = mixer_in_long(x16, w_a_in, w_a_conv, 0, n_p, seq)
    g_s, conv_sample = mixer_in_short(x16, w_a_in, w_a_conv, state_conv, 0, rows_p, n_s, dec_seq)
    x32, x16 = proj_res_ln(jnp.concatenate([g, g_s], axis=0), cast_weight(w_a_out, (0,)), x32, ln_g[0, 1], ln_b[0, 1],
                           alpha=alpha, coef=1.0)
    x32, x16 = ffn(x32, x16, 0, 1, 2)

    w_kv16 = w_kv.astype(BF16)
    k_prompt, v_prompt, k16_p, v16_p = kv_proj(x16, w_kv16, 0, rows_p, n_heads, head_dim)
    k_sample, v_sample, k16_s, v16_s = kv_proj(x16, w_kv16, rows_p, rows_s, n_heads, head_dim)

    x32, x16 = ffn(x32, x16, 1, 0, 0)
    q16 = q_proj(x16, w_q, 0, -math.log2(math.e) * head_dim ** -0.5)
    o = sb_attention_prompt(q16, k16_p, v16_p, n_p, seq, n_heads, head_dim)

    q_hm = (q16[rows_p:].reshape(n_s, dec_seq, n_heads, head_dim)
            .transpose(0, 2, 1, 3).reshape(n_s, n_heads * dec_seq, head_dim))
    pad = ((0, 0), (0, page - dec_seq), (0, 0))
    o_hm = sb_attention_sample(q_hm, jnp.pad(k16_s.reshape(n_s, dec_seq, d), pad),
                               jnp.pad(v16_s.reshape(n_s, dec_seq, d), pad),
                               cache_k, cache_v, page_table, dec_seq)
    o_s = o_hm.reshape(n_s, n_heads, dec_seq, head_dim).transpose(0, 2, 1, 3).reshape(rows_s, d)

    x32, x16 = proj_res_ln(jnp.concatenate([o, o_s], axis=0), cast_weight(w_o, (0,)), x32, ln_g[1, 1], ln_b[1, 1],
                           alpha=alpha, coef=1.0)
    y_prompt, y_sample = ffn(x32, x16, 1, 1, 2, row_splits=((0, rows_p), (rows_p, rows_s)))

    return (y_prompt.reshape(n_p, seq, d), y_sample.reshape(n_s, dec_seq, d),
            conv_prompt[None], conv_sample[None],
            k_prompt.reshape(n_p, seq, n_heads, head_dim), v_prompt.reshape(n_p, seq, n_heads, head_dim),
            k_sample.reshape(n_s, dec_seq, n_heads, head_dim),
            v_sample.reshape(n_s, dec_seq, n_heads, head_dim))
```

```python
import functools
import math

import jax
import jax.numpy as jnp
from jax import lax
from jax.experimental import pallas as pl
from jax.experimental.pallas import tpu as pltpu

F32 = jnp.float32
BF16 = jnp.bfloat16

LN_EPS = 1e-5
CONV_W = 3
SUBLANES = 8
VMEM_LIMIT_BYTES = 56 * 1024 * 1024
ALL_WEIGHTS_ZERO_LOG2 = -160.0


def _params():
    return pltpu.CompilerParams(vmem_limit_bytes=VMEM_LIMIT_BYTES)


def _pick_tile(total, candidates):
    for c in candidates:
        if total % c == 0:
            return c
    return total


def _cast_once(step, pairs):
    @pl.when(step == 0)
    def _():
        for src, dst in pairs:
            dst[...] = src[...].astype(BF16)


def _cast_kernel(w_ref, o_ref):
    o_ref[...] = w_ref[...].astype(BF16)


def cast_weight(w_stack, lead):
    k_dim, n_dim = w_stack.shape[-2:]
    tk = _pick_tile(k_dim, (512, 256, 128))
    nones = (None,) * len(lead)
    return pl.pallas_call(
        _cast_kernel,
        grid=(k_dim // tk,),
        in_specs=[pl.BlockSpec(nones + (tk, n_dim), lambda k: lead + (k, 0))],
        out_specs=pl.BlockSpec((tk, n_dim), lambda k: (k, 0)),
        out_shape=jax.ShapeDtypeStruct((k_dim, n_dim), BF16),
        compiler_params=_params(),
        name="cast_weight",
    )(w_stack)


def _swiglu_in_kernel(x_ref, wa_ref, wg_ref, o_ref, wa16_ref, wg16_ref):
    _cast_once(pl.program_id(1), ((wa_ref, wa16_ref), (wg_ref, wg16_ref)))
    x = x_ref[...]
    a = jnp.dot(x, wa16_ref[...], preferred_element_type=F32)
    g = jnp.dot(x, wg16_ref[...], preferred_element_type=F32)
    o_ref[...] = (a * jax.nn.sigmoid(a) * g).astype(o_ref.dtype)


def swiglu_in(x16, w_ffn_in, layer, half):
    m_rows, d = x16.shape
    f = w_ffn_in.shape[-1] // 2
    tm = _pick_tile(m_rows, (1056, 1024, 512, 256, 128))
    tn = _pick_tile(f, (512, 256, 128))
    n_blocks = f // tn
    return pl.pallas_call(
        _swiglu_in_kernel,
        grid=(n_blocks, m_rows // tm),
        in_specs=[
            pl.BlockSpec((tm, d), lambda n, m: (m, 0)),
            pl.BlockSpec((None, None, d, tn), lambda n, m: (layer, half, 0, n)),
            pl.BlockSpec((None, None, d, tn), lambda n, m: (layer, half, 0, n + n_blocks)),
        ],
        out_specs=pl.BlockSpec((tm, tn), lambda n, m: (m, n)),
        out_shape=jax.ShapeDtypeStruct((m_rows, f), BF16),
        scratch_shapes=[pltpu.VMEM((d, tn), BF16)] * 2,
        compiler_params=_params(),
        name="swiglu_in",
    )(x16, w_ffn_in, w_ffn_in)


def _proj_res_ln_kernel(a_ref, w_ref, res_ref, gain_ref, bias_ref, *out_refs, alpha, coef, chunk):
    for r in range(a_ref.shape[0] // chunk):
        rows = slice(r * chunk, (r + 1) * chunk)
        y = alpha * res_ref[rows, :] + coef * jnp.dot(a_ref[rows, :], w_ref[...],
                                                      preferred_element_type=F32)
        mu = jnp.mean(y, axis=-1, keepdims=True)
        dev = y - mu
        var = jnp.mean(dev * dev, axis=-1, keepdims=True)
        out = dev * lax.rsqrt(var + LN_EPS) * gain_ref[...] + bias_ref[...]
        for o_ref in out_refs:
            o_ref[rows, :] = out.astype(o_ref.dtype)


def proj_res_ln(a16, w16, res32, gain, bias, *, alpha, coef, row_start=0, n_rows=None,
                out_dtypes=(F32, BF16)):
    k_dim = a16.shape[1]
    d = w16.shape[1]
    n_rows = a16.shape[0] if n_rows is None else n_rows
    tiles = (528, 512, 256, 128) if k_dim <= d else (384, 256, 128)
    tm = _pick_tile(math.gcd(n_rows, row_start) if row_start else n_rows, tiles)
    chunk = _pick_tile(tm, (192, 176, 128))
    m0 = row_start // tm
    kern = functools.partial(_proj_res_ln_kernel, alpha=alpha, coef=coef, chunk=chunk)
    return pl.pallas_call(
        kern,
        grid=(n_rows // tm,),
        in_specs=[
            pl.BlockSpec((tm, k_dim), lambda m: (m + m0, 0)),
            pl.BlockSpec((k_dim, d), lambda m: (0, 0), pipeline_mode=pl.Buffered(1)),
            pl.BlockSpec((tm, d), lambda m: (m + m0, 0)),
            pl.BlockSpec((1, d), lambda m: (0, 0)),
            pl.BlockSpec((1, d), lambda m: (0, 0)),
        ],
        out_specs=[pl.BlockSpec((tm, d), lambda m: (m, 0)) for _ in out_dtypes],
        out_shape=[jax.ShapeDtypeStruct((n_rows, d), dt) for dt in out_dtypes],
        compiler_params=_params(),
        name="proj_res_ln",
    )(a16, w16, res32, gain.reshape(1, d), bias.reshape(1, d))


def _conv_taps(wconv_ref):
    wc = wconv_ref[...]
    return wc[0:1], wc[1:2], wc[2:3]


def _gates(x_ref, w16_refs):
    x = x_ref[...]
    b_gate, c_gate, h = (jnp.dot(x, w[...], preferred_element_type=F32) for w in w16_refs)
    return b_gate, c_gate * h


def _mixer_in_long_kernel(x_ref, wb_ref, wc_ref, wh_ref, wconv_ref, g_ref, st_ref,
                          wb16_ref, wc16_ref, wh16_ref, hist_ref, *, tiles_per_seq, tm):
    m = pl.program_id(1)
    w16 = (wb16_ref, wc16_ref, wh16_ref)
    _cast_once(m, zip((wb_ref, wc_ref, wh_ref), w16))
    b_gate, u = _gates(x_ref, w16)

    @pl.when(m % tiles_per_seq == 0)
    def _():
        hist_ref[0:SUBLANES, :] = jnp.zeros((SUBLANES, hist_ref.shape[1]), F32)

    hist_ref[SUBLANES:SUBLANES + tm, :] = u
    w0, w1, w2 = _conv_taps(wconv_ref)
    conv = (w0 * hist_ref[SUBLANES - 2:SUBLANES - 2 + tm, :]
            + w1 * hist_ref[SUBLANES - 1:SUBLANES - 1 + tm, :]
            + w2 * u)
    g_ref[...] = (b_gate * conv).astype(g_ref.dtype)
    st_ref[0] = u[tm - (CONV_W - 1):tm, :]
    hist_ref[0:SUBLANES, :] = u[tm - SUBLANES:tm, :]


def _mixer_w_specs(d, tn, n_blocks, index):
    return [pl.BlockSpec((None, d, tn), index(j * n_blocks)) for j in range(3)]


def mixer_in_long(x16, w_a_in, w_a_conv, layer, n_seq, seq_len):
    d = x16.shape[1]
    tm = _pick_tile(seq_len, (512, 256, 128))
    tn = _pick_tile(d, (512, 256, 128))
    n_blocks = d // tn
    tiles_per_seq = seq_len // tm
    kern = functools.partial(_mixer_in_long_kernel, tiles_per_seq=tiles_per_seq, tm=tm)
    return pl.pallas_call(
        kern,
        grid=(n_blocks, n_seq * tiles_per_seq),
        in_specs=[pl.BlockSpec((tm, d), lambda n, m: (m, 0))]
        + _mixer_w_specs(d, tn, n_blocks, lambda off: (lambda n, m: (layer, 0, n + off)))
        + [pl.BlockSpec((None, CONV_W, tn), lambda n, m: (layer, 0, n))],
        out_specs=[
            pl.BlockSpec((tm, tn), lambda n, m: (m, n)),
            pl.BlockSpec((1, CONV_W - 1, tn), lambda n, m: (m // tiles_per_seq, 0, n)),
        ],
        out_shape=[jax.ShapeDtypeStruct((n_seq * seq_len, d), BF16),
                   jax.ShapeDtypeStruct((n_seq, CONV_W - 1, d), F32)],
        scratch_shapes=[pltpu.VMEM((d, tn), BF16)] * 3 + [pltpu.VMEM((tm + SUBLANES, tn), F32)],
        compiler_params=_params(),
        name="mixer_in_long",
    )(x16, w_a_in, w_a_in, w_a_in, w_a_conv)


def _mixer_in_short_kernel(x_ref, wb_ref, wc_ref, wh_ref, wconv_ref, prev_ref, g_ref, st_ref,
                           *, seq_len):
    rows, tn = x_ref.shape[0], wb_ref.shape[1]
    n_seq = rows // seq_len
    x = x_ref[...]
    b_gate, c_gate, h = (jnp.dot(x, w[...].astype(BF16), preferred_element_type=F32)
                         for w in (wb_ref, wc_ref, wh_ref))
    u3 = (c_gate * h).reshape(n_seq, seq_len, tn)
    prev = prev_ref[...]
    p0, p1 = prev[:, 0:1, :], prev[:, 1:2, :]
    pos = lax.broadcasted_iota(jnp.int32, u3.shape, 1)
    u1 = jnp.where(pos == 0, p1, pltpu.roll(u3, 1, axis=1))
    u2 = jnp.where(pos == 0, p0, jnp.where(pos == 1, p1, pltpu.roll(u3, 2, axis=1)))
    w0, w1, w2 = _conv_taps(wconv_ref)
    conv = w0 * u2 + w1 * u1 + w2 * u3
    g_ref[...] = (b_gate * conv.reshape(rows, tn)).astype(g_ref.dtype)
    st_ref[...] = u3[:, seq_len - (CONV_W - 1):seq_len, :]


def mixer_in_short(x16, w_a_in, w_a_conv, prev, layer, row_start, n_seq, seq_len):
    d = x16.shape[1]
    rows = n_seq * seq_len
    assert seq_len == SUBLANES and row_start % rows == 0
    tn = _pick_tile(d, (512, 256, 128))
    n_blocks = d // tn
    kern = functools.partial(_mixer_in_short_kernel, seq_len=seq_len)
    return pl.pallas_call(
        kern,
        grid=(n_blocks,),
        in_specs=[pl.BlockSpec((rows, d), lambda n: (row_start // rows, 0))]
        + _mixer_w_specs(d, tn, n_blocks, lambda off: (lambda n: (layer, 0, n + off)))
        + [pl.BlockSpec((None, CONV_W, tn), lambda n: (layer, 0, n)),
           pl.BlockSpec((None, n_seq, CONV_W - 1, tn), lambda n: (layer, 0, 0, n))],
        out_specs=[
            pl.BlockSpec((rows, tn), lambda n: (0, n)),
            pl.BlockSpec((n_seq, CONV_W - 1, tn), lambda n: (0, 0, n)),
        ],
        out_shape=[jax.ShapeDtypeStruct((rows, d), BF16),
                   jax.ShapeDtypeStruct((n_seq, CONV_W - 1, d), F32)],
        compiler_params=_params(),
        name="mixer_in_short",
    )(x16, w_a_in, w_a_in, w_a_in, w_a_conv, prev)


def _kv_proj_kernel(x_ref, wk_ref, wv_ref, k32_ref, v32_ref, k16_ref, v16_ref, *, head_dim):
    x = x_ref[...]
    tm, heads = k32_ref.shape[:2]
    for w_ref, o32_ref, o16_ref in ((wk_ref, k32_ref, k16_ref), (wv_ref, v32_ref, v16_ref)):
        y = jnp.dot(x, w_ref[...], preferred_element_type=F32)
        o16_ref[...] = y.astype(BF16)
        rows = o32_ref.reshape(tm * heads, head_dim)
        for h in range(heads):
            rows[pl.ds(h, tm, stride=heads), :] = y[:, h * head_dim:(h + 1) * head_dim]


def kv_proj(x16, w_kv16, row_start, n_rows, n_heads, head_dim):
    d = x16.shape[1]
    tm = _pick_tile(math.gcd(n_rows, row_start) if row_start else n_rows, (512, 256, 128))
    heads_per_block = SUBLANES
    tn = heads_per_block * head_dim
    n_blocks = d // tn
    m0 = row_start // tm
    spec32 = pl.BlockSpec((tm, heads_per_block, head_dim), lambda n, m: (m, n, 0))
    spec16 = pl.BlockSpec((tm, tn), lambda n, m: (m, n))
    return pl.pallas_call(
        functools.partial(_kv_proj_kernel, head_dim=head_dim),
        grid=(n_blocks, n_rows // tm),
        in_specs=[
            pl.BlockSpec((tm, d), lambda n, m: (m + m0, 0)),
            pl.BlockSpec((d, tn), lambda n, m: (0, n)),
            pl.BlockSpec((d, tn), lambda n, m: (0, n + n_blocks)),
        ],
        out_specs=[spec32, spec32, spec16, spec16],
        out_shape=[jax.ShapeDtypeStruct((n_rows, n_heads, head_dim), F32)] * 2
                  + [jax.ShapeDtypeStruct((n_rows, d), BF16)] * 2,
        compiler_params=_params(),
        name="kv_proj",
    )(x16, w_kv16, w_kv16)


def _q_proj_kernel(x_ref, w_ref, o_ref, w16_ref, *, scale):
    _cast_once(pl.program_id(1), ((w_ref, w16_ref),))
    q = jnp.dot(x_ref[...], w16_ref[...], preferred_element_type=F32)
    o_ref[...] = (q * scale).astype(o_ref.dtype)


def q_proj(x16, w_q, layer, scale):
    m_rows, d = x16.shape
    tm = _pick_tile(m_rows, (1056, 1024, 512, 256, 128))
    tn = _pick_tile(d, (512, 256, 128))
    return pl.pallas_call(
        functools.partial(_q_proj_kernel, scale=scale),
        grid=(d // tn, m_rows // tm),
        in_specs=[
            pl.BlockSpec((tm, d), lambda n, m: (m, 0)),
            pl.BlockSpec((None, d, tn), lambda n, m: (layer, 0, n)),
        ],
        out_specs=pl.BlockSpec((tm, tn), lambda n, m: (m, n)),
        out_shape=jax.ShapeDtypeStruct((m_rows, d), BF16),
        scratch_shapes=[pltpu.VMEM((d, tn), BF16)],
        compiler_params=_params(),
        name="q_proj",
    )(x16, w_q)


def _suffix_matrix(n):
    j = lax.broadcasted_iota(jnp.int32, (2 * n, n), 0) % n
    s = lax.broadcasted_iota(jnp.int32, (2 * n, n), 1)
    return (j > s).astype(BF16)


def _log2_keep(zn, causal):
    lk = jnp.minimum(zn, 0.0) - jnp.log2(1.0 + jnp.exp2(-jnp.abs(zn)))
    return lk if causal is None else jnp.where(causal, lk, 0.0)


def _stick_weights(zn, lk, suffix, carry):
    hi = lk.astype(BF16)
    lo = (lk - hi.astype(F32)).astype(BF16)
    after = jnp.dot(jnp.concatenate([hi, lo], axis=1), suffix, preferred_element_type=F32)
    w = jnp.exp2((lk - zn) + after + carry)
    return w, carry + jnp.sum(lk, axis=-1, keepdims=True)


_NT = (((1,), (1,)), ((), ()))


def _sb_prompt_kernel(q_ref, k_ref, v_ref, suffix_ref, o_ref, *, tq, sub, n_sub):
    qi = pl.program_id(2)
    q = q_ref[...]
    suffix = suffix_ref[...]
    head_dim = q.shape[1]
    step = sub * n_sub

    def visit(key_start, carry, acc, causal):
        k = k_ref[pl.ds(key_start, step), :]
        v = v_ref[pl.ds(key_start, step), :]
        zn = lax.dot_general(q, k, _NT, preferred_element_type=F32)
        lk = _log2_keep(zn, causal)
        ws = [None] * n_sub
        for s in reversed(range(n_sub)):
            cols = slice(s * sub, (s + 1) * sub)
            ws[s], carry = _stick_weights(zn[:, cols], lk[:, cols], suffix, carry)
        w = jnp.concatenate(ws, axis=1)
        if causal is not None:
            w = jnp.where(causal, w, 0.0)
        return carry, acc + jnp.dot(w.astype(BF16), v, preferred_element_type=F32)

    carry = jnp.zeros((tq, 1), F32)
    acc = jnp.zeros((tq, head_dim), F32)
    t = lax.broadcasted_iota(jnp.int32, (tq, step), 0)
    s = lax.broadcasted_iota(jnp.int32, (tq, step), 1)
    n_diag = tq // step
    for d in reversed(range(n_diag)):
        carry, acc = visit(pl.multiple_of(qi * tq + d * step, step), carry, acc, s + d * step < t)

    n_older = qi * n_diag

    def any_alive(carry):
        return jnp.max(carry) > ALL_WEIGHTS_ZERO_LOG2

    def cond(loop):
        i, alive, _, _ = loop
        return jnp.logical_and(i < n_older, alive)

    def body(loop):
        i, _, carry, acc = loop
        carry, acc = visit(pl.multiple_of((n_older - 1 - i) * step, step), carry, acc, None)
        return i + 1, any_alive(carry), carry, acc

    _, _, _, acc = lax.while_loop(cond, body, (0, any_alive(carry), carry, acc))
    o_ref[...] = acc.astype(o_ref.dtype)


def sb_attention_prompt(q16, k16, v16, n_seq, seq_len, n_heads, head_dim):
    tq = _pick_tile(seq_len, (512, 256, 128))
    sub = min(256, tq)
    n_sub = tq // sub
    n_q = seq_len // tq
    kern = functools.partial(_sb_prompt_kernel, tq=tq, sub=sub, n_sub=n_sub)
    return pl.pallas_call(
        kern,
        grid=(n_seq, n_heads, n_q),
        in_specs=[
            pl.BlockSpec((tq, head_dim), lambda b, h, i: (b * n_q + i, h)),
            pl.BlockSpec((seq_len, head_dim), lambda b, h, i: (b, h)),
            pl.BlockSpec((seq_len, head_dim), lambda b, h, i: (b, h)),
            pl.BlockSpec((2 * sub, sub), lambda b, h, i: (0, 0)),
        ],
        out_specs=pl.BlockSpec((tq, head_dim), lambda b, h, i: (b * n_q + i, h)),
        out_shape=jax.ShapeDtypeStruct(k16.shape, BF16),
        compiler_params=_params(),
        name="sb_attention_prompt",
    )(q16, k16, v16, _suffix_matrix(sub))


def _sb_sample_kernel(pt_ref, q_ref, knew_ref, vnew_ref, ck_ref, cv_ref, suffix_ref, o_ref,
                      kbuf, vbuf, sem, acc_ref, carry_ref, kt_ref, *, pages_per_step, n_heads, dec_seq):
    c = pl.program_id(1)
    n_steps = pl.num_programs(1)
    t = pl.program_id(0) * n_steps + c
    last = pl.num_programs(0) * n_steps - 1
    slot = t % 2
    head_dim = q_ref.shape[2]
    half_heads = n_heads // 2
    page = kbuf.shape[3]
    rows_per_pair = 2 * dec_seq
    suffix = suffix_ref[...]

    def page_copies(step, dst_slot):
        seq, grp = step // n_steps, step % n_steps
        copies = []
        for i in range(pages_per_step):
            pid = pt_ref[seq, (n_steps - 1 - grp) * pages_per_step + (pages_per_step - 1 - i)]
            keys = pl.ds(pl.multiple_of(pid * page, page), page)
            for half in range(2):
                heads = pl.ds(half * half_heads, half_heads)
                for src, dst in ((ck_ref, kbuf), (cv_ref, vbuf)):
                    copies.append(pltpu.make_async_copy(
                        src.at[keys, heads, :], dst.at[dst_slot, i, half], sem.at[dst_slot]))
        return copies

    @pl.when(t == 0)
    def _():
        for cp in page_copies(t, slot):
            cp.start()

    @pl.when(t < last)
    def _():
        for cp in page_copies(t + 1, 1 - slot):
            cp.start()

    def visit(k_head, v_head, causal):
        zs = []
        for hp in range(n_heads // 2):
            lhs = q_ref[0, hp * rows_per_pair:(hp + 1) * rows_per_pair, :]
            for half in range(2):
                h = 2 * hp + half
                kt_ref[h] = k_head(h).T
                z_h = jnp.dot(lhs, kt_ref[h].astype(BF16), preferred_element_type=F32)
                zs.append(z_h[half * dec_seq:(half + 1) * dec_seq])
        zn = jnp.concatenate(zs, axis=0)
        w, carry = _stick_weights(zn, _log2_keep(zn, causal), suffix, carry_ref[...])
        if causal is not None:
            w = jnp.where(causal, w, 0.0)
        carry_ref[...] = carry
        w16 = w.astype(BF16)
        outs = []
        for hp in range(n_heads // 2):
            lhs = w16[hp * rows_per_pair:(hp + 1) * rows_per_pair, :]
            for half in range(2):
                o_h = jnp.dot(lhs, v_head(2 * hp + half).astype(BF16), preferred_element_type=F32)
                outs.append(o_h[half * dec_seq:(half + 1) * dec_seq])
        acc_ref[...] += jnp.concatenate(outs, axis=0)

    def lane_slab(ref):
        return lambda h: ref[0, :, h * head_dim:(h + 1) * head_dim].astype(F32)

    def strided_rows(buf, i):
        def get(h):
            rows = buf.at[slot, i, h // half_heads].reshape(page * half_heads, head_dim)
            return rows[pl.ds(h % half_heads, page, stride=half_heads), :]
        return get

    @pl.when(c == 0)
    def _():
        acc_ref[...] = jnp.zeros_like(acc_ref)
        carry_ref[...] = jnp.zeros_like(carry_ref)
        n_new = knew_ref.shape[1]
        row = lax.broadcasted_iota(jnp.int32, (n_heads * dec_seq, n_new), 0)
        col = lax.broadcasted_iota(jnp.int32, (n_heads * dec_seq, n_new), 1)
        visit(lane_slab(knew_ref), lane_slab(vnew_ref), col < (row % dec_seq))

    for cp in page_copies(t, slot):
        cp.wait()

    for i in range(pages_per_step):
        visit(strided_rows(kbuf, i), strided_rows(vbuf, i), None)

    @pl.when(c == n_steps - 1)
    def _():
        o_ref[0] = acc_ref[...].astype(o_ref.dtype)


def sb_attention_sample(q_hm, knew_pad, vnew_pad, cache_k, cache_v, page_table, dec_seq):
    n_seq, rows, head_dim = q_hm.shape
    n_phys, page, n_heads, _ = cache_k.shape
    n_pages = page_table.shape[1]
    pages_per_step = _pick_tile(n_pages, (8, 4, 2, 1))
    n_steps = n_pages // pages_per_step
    assert n_heads % (2 * SUBLANES) == 0
    half_heads = n_heads // 2
    kern = functools.partial(_sb_sample_kernel, pages_per_step=pages_per_step,
                             n_heads=n_heads, dec_seq=dec_seq)
    page_buffers = pltpu.VMEM((2, pages_per_step, 2, page, half_heads, head_dim), F32)
    grid_spec = pltpu.PrefetchScalarGridSpec(
        num_scalar_prefetch=1,
        grid=(n_seq, n_steps),
        in_specs=[
            pl.BlockSpec((1, rows, head_dim), lambda b, c, pt: (b, 0, 0)),
            pl.BlockSpec((1, page, n_heads * head_dim), lambda b, c, pt: (b, 0, 0)),
            pl.BlockSpec((1, page, n_heads * head_dim), lambda b, c, pt: (b, 0, 0)),
            pl.BlockSpec(memory_space=pl.ANY),
            pl.BlockSpec(memory_space=pl.ANY),
            pl.BlockSpec((2 * page, page), lambda b, c, pt: (0, 0)),
        ],
        out_specs=pl.BlockSpec((1, rows, head_dim), lambda b, c, pt: (b, 0, 0)),
        scratch_shapes=[page_buffers, page_buffers, pltpu.SemaphoreType.DMA((2,)),
                        pltpu.VMEM((rows, head_dim), F32), pltpu.VMEM((rows, 1), F32),
                        pltpu.VMEM((n_heads, head_dim, page), F32)],
    )
    ck = cache_k.reshape(n_phys * page, n_heads, head_dim)
    cv = cache_v.reshape(n_phys * page, n_heads, head_dim)
    return pl.pallas_call(
        kern,
        grid_spec=grid_spec,
        out_shape=jax.ShapeDtypeStruct((n_seq, rows, head_dim), BF16),
        compiler_params=pltpu.CompilerParams(vmem_limit_bytes=VMEM_LIMIT_BYTES,
                                             dimension_semantics=("arbitrary", "arbitrary")),
        name="sb_attention_sample",
    )(page_table, q_hm, knew_pad, vnew_pad, ck, cv, _suffix_matrix(page))


def kernel(x_prompt, x_sample, state_conv, cache_k, cache_v, page_table, ln_g, ln_b, w_ffn_in,
           w_ffn_out, w_a_in, w_a_conv, w_a_out, w_kv, w_q, w_o):
    n_p, seq, d = x_prompt.shape
    n_s, dec_seq, _ = x_sample.shape
    depth = ln_g.shape[0]
    page, n_heads, head_dim = cache_k.shape[1:]
    rows_p, rows_s = n_p * seq, n_s * dec_seq
    alpha = (2.0 * depth) ** 0.25
    assert w_a_in.shape[0] == 1 and depth == 2, "one short-conv layer followed by one attention layer"

    x32 = jnp.concatenate([x_prompt.reshape(rows_p, d), x_sample.reshape(rows_s, d)], axis=0)
    x16 = x32.astype(BF16)

    def ffn(x32, x16, layer, half, ln_idx, row_splits=None):
        h = swiglu_in(x16, w_ffn_in, layer, half)
        w_out16 = cast_weight(w_ffn_out, (layer, half))
        norm = functools.partial(proj_res_ln, h, w_out16, x32, ln_g[layer, ln_idx], ln_b[layer, ln_idx],
                                 alpha=alpha, coef=0.5)
        if row_splits is None:
            return norm()
        return [norm(row_start=r0, n_rows=n, out_dtypes=(F32,))[0] for r0, n in row_splits]

    x32, x16 = ffn(x32, x16, 0, 0, 0)
    g, conv_prompt = mixer_in_long(x16, w_a_in, w_a_conv, 0, n_p, seq)
    g_s, conv_sample = mixer_in_short(x16, w_a_in, w_a_conv, state_conv, 0, rows_p, n_s, dec_seq)
    x32, x16 = proj_res_ln(jnp.concatenate([g, g_s], axis=0), cast_weight(w_a_out, (0,)), x32, ln_g[0, 1], ln_b[0, 1],
                           alpha=alpha, coef=1.0)
    x32, x16 = ffn(x32, x16, 0, 1, 2)

    w_kv16 = w_kv.astype(BF16)
    k_prompt, v_prompt, k16_p, v16_p = kv_proj(x16, w_kv16, 0, rows_p, n_heads, head_dim)
    k_sample, v_sample, k16_s, v16_s = kv_proj(x16, w_kv16, rows_p, rows_s, n_heads, head_dim)

    x32, x16 = ffn(x32, x16, 1, 0, 0)
    q16 = q_proj(x16, w_q, 0, -math.log2(math.e) * head_dim ** -0.5)
    o = sb_attention_prompt(q16, k16_p, v16_p, n_p, seq, n_heads, head_dim)

    q_hm = (q16[rows_p:].reshape(n_s, dec_seq, n_heads, head_dim)
            .transpose(0, 2, 1, 3).reshape(n_s, n_heads * dec_seq, head_dim))
    pad = ((0, 0), (0, page - dec_seq), (0, 0))
    o_hm = sb_attention_sample(q_hm, jnp.pad(k16_s.reshape(n_s, dec_seq, d), pad),
                               jnp.pad(v16_s.reshape(n_s, dec_seq, d), pad),
                               cache_k, cache_v, page_table, dec_seq)
    o_s = o_hm.reshape(n_s, n_heads, dec_seq, head_dim).transpose(0, 2, 1, 3).reshape(rows_s, d)

    x32, x16 = proj_res_ln(jnp.concatenate([o, o_s], axis=0), cast_weight(w_o, (0,)), x32, ln_g[1, 1], ln_b[1, 1],
                           alpha=alpha, coef=1.0)
    y_prompt, y_sample = ffn(x32, x16, 1, 1, 2, row_splits=((0, rows_p), (rows_p, rows_s)))

    return (y_prompt.reshape(n_p, seq, d), y_sample.reshape(n_s, dec_seq, d),
            conv_prompt[None], conv_sample[None],
            k_prompt.reshape(n_p, seq, n_heads, head_dim), v_prompt.reshape(n_p, seq, n_heads, head_dim),
            k_sample.reshape(n_s, dec_seq, n_heads, head_dim),
            v_sample.reshape(n_s, dec_seq, n_heads, head_dim))
```

```python
import functools
import math

import jax
import jax.numpy as jnp
from jax import lax
from jax.experimental import pallas as pl
from jax.experimental.pallas import tpu as pltpu

F32 = jnp.float32
BF16 = jnp.bfloat16

LN_EPS = 1e-5
CONV_W = 3
SUBLANES = 8
VMEM_LIMIT_BYTES = 56 * 1024 * 1024
ALL_WEIGHTS_ZERO_LOG2 = -160.0


def _params():
    return pltpu.CompilerParams(vmem_limit_bytes=VMEM_LIMIT_BYTES)


def _pick_tile(total, candidates):
    for c in candidates:
        if total % c == 0:
            return c
    return total


def _cast_once(step, pairs):
    @pl.when(step == 0)
    def _():
        for src, dst in pairs:
            dst[...] = src[...].astype(BF16)


def _cast_kernel(w_ref, o_ref):
    o_ref[...] = w_ref[...].astype(BF16)


def cast_weight(w_stack, lead):
    k_dim, n_dim = w_stack.shape[-2:]
    tk = _pick_tile(k_dim, (512, 256, 128))
    nones = (None,) * len(lead)
    return pl.pallas_call(
        _cast_kernel,
        grid=(k_dim // tk,),
        in_specs=[pl.BlockSpec(nones + (tk, n_dim), lambda k: lead + (k, 0))],
        out_specs=pl.BlockSpec((tk, n_dim), lambda k: (k, 0)),
        out_shape=jax.ShapeDtypeStruct((k_dim, n_dim), BF16),
        compiler_params=_params(),
        name="cast_weight",
    )(w_stack)


def _swiglu_in_kernel(x_ref, wa_ref, wg_ref, o_ref, wa16_ref, wg16_ref):
    _cast_once(pl.program_id(1), ((wa_ref, wa16_ref), (wg_ref, wg16_ref)))
    x = x_ref[...]
    a = jnp.dot(x, wa16_ref[...], preferred_element_type=F32)
    g = jnp.dot(x, wg16_ref[...], preferred_element_type=F32)
    o_ref[...] = (a * jax.nn.sigmoid(a) * g).astype(o_ref.dtype)


def swiglu_in(x16, w_ffn_in, layer, half):
    m_rows, d = x16.shape
    f = w_ffn_in.shape[-1] // 2
    tm = _pick_tile(m_rows, (1056, 1024, 512, 256, 128))
    tn = _pick_tile(f, (512, 256, 128))
    n_blocks = f // tn
    return pl.pallas_call(
        _swiglu_in_kernel,
        grid=(n_blocks, m_rows // tm),
        in_specs=[
            pl.BlockSpec((tm, d), lambda n, m: (m, 0)),
            pl.BlockSpec((None, None, d, tn), lambda n, m: (layer, half, 0, n)),
            pl.BlockSpec((None, None, d, tn), lambda n, m: (layer, half, 0, n + n_blocks)),
        ],
        out_specs=pl.BlockSpec((tm, tn), lambda n, m: (m, n)),
        out_shape=jax.ShapeDtypeStruct((m_rows, f), BF16),
        scratch_shapes=[pltpu.VMEM((d, tn), BF16)] * 2,
        compiler_params=_params(),
        name="swiglu_in",
    )(x16, w_ffn_in, w_ffn_in)


def _proj_res_ln_kernel(a_ref, w_ref, res_ref, gain_ref, bias_ref, *out_refs, alpha, coef, chunk):
    for r in range(a_ref.shape[0] // chunk):
        rows = slice(r * chunk, (r + 1) * chunk)
        y = alpha * res_ref[rows, :] + coef * jnp.dot(a_ref[rows, :], w_ref[...],
                                                      preferred_element_type=F32)
        mu = jnp.mean(y, axis=-1, keepdims=True)
        dev = y - mu
        var = jnp.mean(dev * dev, axis=-1, keepdims=True)
        out = dev * lax.rsqrt(var + LN_EPS) * gain_ref[...] + bias_ref[...]
        for o_ref in out_refs:
            o_ref[rows, :] = out.astype(o_ref.dtype)


def proj_res_ln(a16, w16, res32, gain, bias, *, alpha, coef, row_start=0, n_rows=None,
                out_dtypes=(F32, BF16)):
    k_dim = a16.shape[1]
    d = w16.shape[1]
    n_rows = a16.shape[0] if n_rows is None else n_rows
    tiles = (528, 512, 256, 128) if k_dim <= d else (384, 256, 128)
    tm = _pick_tile(math.gcd(n_rows, row_start) if row_start else n_rows, tiles)
    chunk = _pick_tile(tm, (192, 176, 128))
    m0 = row_start // tm
    kern = functools.partial(_proj_res_ln_kernel, alpha=alpha, coef=coef, chunk=chunk)
    return pl.pallas_call(
        kern,
        grid=(n_rows // tm,),
        in_specs=[
            pl.BlockSpec((tm, k_dim), lambda m: (m + m0, 0)),
            pl.BlockSpec((k_dim, d), lambda m: (0, 0), pipeline_mode=pl.Buffered(1)),
            pl.BlockSpec((tm, d), lambda m: (m + m0, 0)),
            pl.BlockSpec((1, d), lambda m: (0, 0)),
            pl.BlockSpec((1, d), lambda m: (0, 0)),
        ],
        out_specs=[pl.BlockSpec((tm, d), lambda m: (m, 0)) for _ in out_dtypes],
        out_shape=[jax.ShapeDtypeStruct((n_rows, d), dt) for dt in out_dtypes],
        compiler_params=_params(),
        name="proj_res_ln",
    )(a16, w16, res32, gain.reshape(1, d), bias.reshape(1, d))


def _conv_taps(wconv_ref):
    wc = wconv_ref[...]
    return wc[0:1], wc[1:2], wc[2:3]


def _gates(x_ref, w16_refs):
    x = x_ref[...]
    b_gate, c_gate, h = (jnp.dot(x, w[...], preferred_element_type=F32) for w in w16_refs)
    return b_gate, c_gate * h


def _mixer_in_long_kernel(x_ref, wb_ref, wc_ref, wh_ref, wconv_ref, g_ref, st_ref,
                          wb16_ref, wc16_ref, wh16_ref, hist_ref, *, tiles_per_seq, tm):
    m = pl.program_id(1)
    w16 = (wb16_ref, wc16_ref, wh16_ref)
    _cast_once(m, zip((wb_ref, wc_ref, wh_ref), w16))
    b_gate, u = _gates(x_ref, w16)

    @pl.when(m % tiles_per_seq == 0)
    def _():
        hist_ref[0:SUBLANES, :] = jnp.zeros((SUBLANES, hist_ref.shape[1]), F32)

    hist_ref[SUBLANES:SUBLANES + tm, :] = u
    w0, w1, w2 = _conv_taps(wconv_ref)
    conv = (w0 * hist_ref[SUBLANES - 2:SUBLANES - 2 + tm, :]
            + w1 * hist_ref[SUBLANES - 1:SUBLANES - 1 + tm, :]
            + w2 * u)
    g_ref[...] = (b_gate * conv).astype(g_ref.dtype)
    st_ref[0] = u[tm - (CONV_W - 1):tm, :]
    hist_ref[0:SUBLANES, :] = u[tm - SUBLANES:tm, :]


def _mixer_w_specs(d, tn, n_blocks, index):
    return [pl.BlockSpec((None, d, tn), index(j * n_blocks)) for j in range(3)]


def mixer_in_long(x16, w_a_in, w_a_conv, layer, n_seq, seq_len):
    d = x16.shape[1]
    tm = _pick_tile(seq_len, (512, 256, 128))
    tn = _pick_tile(d, (512, 256, 128))
    n_blocks = d // tn
    tiles_per_seq = seq_len // tm
    kern = functools.partial(_mixer_in_long_kernel, tiles_per_seq=tiles_per_seq, tm=tm)
    return pl.pallas_call(
        kern,
        grid=(n_blocks, n_seq * tiles_per_seq),
        in_specs=[pl.BlockSpec((tm, d), lambda n, m: (m, 0))]
        + _mixer_w_specs(d, tn, n_blocks, lambda off: (lambda n, m: (layer, 0, n + off)))
        + [pl.BlockSpec((None, CONV_W, tn), lambda n, m: (layer, 0, n))],
        out_specs=[
            pl.BlockSpec((tm, tn), lambda n, m: (m, n)),
            pl.BlockSpec((1, CONV_W - 1, tn), lambda n, m: (m // tiles_per_seq, 0, n)),
        ],
        out_shape=[jax.ShapeDtypeStruct((n_seq * seq_len, d), BF16),
                   jax.ShapeDtypeStruct((n_seq, CONV_W - 1, d), F32)],
        scratch_shapes=[pltpu.VMEM((d, tn), BF16)] * 3 + [pltpu.VMEM((tm + SUBLANES, tn), F32)],
        compiler_params=_params(),
        name="mixer_in_long",
    )(x16, w_a_in, w_a_in, w_a_in, w_a_conv)


def _mixer_in_short_kernel(x_ref, wb_ref, wc_ref, wh_ref, wconv_ref, prev_ref, g_ref, st_ref,
                           *, seq_len):
    rows, tn = x_ref.shape[0], wb_ref.shape[1]
    n_seq = rows // seq_len
    x = x_ref[...]
    b_gate, c_gate, h = (jnp.dot(x, w[...].astype(BF16), preferred_element_type=F32)
                         for w in (wb_ref, wc_ref, wh_ref))
    u3 = (c_gate * h).reshape(n_seq, seq_len, tn)
    prev = prev_ref[...]
    p0, p1 = prev[:, 0:1, :], prev[:, 1:2, :]
    pos = lax.broadcasted_iota(jnp.int32, u3.shape, 1)
    u1 = jnp.where(pos == 0, p1, pltpu.roll(u3, 1, axis=1))
    u2 = jnp.where(pos == 0, p0, jnp.where(pos == 1, p1, pltpu.roll(u3, 2, axis=1)))
    w0, w1, w2 = _conv_taps(wconv_ref)
    conv = w0 * u2 + w1 * u1 + w2 * u3
    g_ref[...] = (b_gate * conv.reshape(rows, tn)).astype(g_ref.dtype)
    st_ref[...] = u3[:, seq_len - (CONV_W - 1):seq_len, :]


def mixer_in_short(x16, w_a_in, w_a_conv, prev, layer, row_start, n_seq, seq_len):
    d = x16.shape[1]
    rows = n_seq * seq_len
    assert seq_len == SUBLANES and row_start % rows == 0
    tn = _pick_tile(d, (512, 256, 128))
    n_blocks = d // tn
    kern = functools.partial(_mixer_in_short_kernel, seq_len=seq_len)
    return pl.pallas_call(
        kern,
        grid=(n_blocks,),
        in_specs=[pl.BlockSpec((rows, d), lambda n: (row_start // rows, 0))]
        + _mixer_w_specs(d, tn, n_blocks, lambda off: (lambda n: (layer, 0, n + off)))
        + [pl.BlockSpec((None, CONV_W, tn), lambda n: (layer, 0, n)),
           pl.BlockSpec((None, n_seq, CONV_W - 1, tn), lambda n: (layer, 0, 0, n))],
        out_specs=[
            pl.BlockSpec((rows, tn), lambda n: (0, n)),
            pl.BlockSpec((n_seq, CONV_W - 1, tn), lambda n: (0, 0, n)),
        ],
        out_shape=[jax.ShapeDtypeStruct((rows, d), BF16),
                   jax.ShapeDtypeStruct((n_seq, CONV_W - 1, d), F32)],
        compiler_params=_params(),
        name="mixer_in_short",
    )(x16, w_a_in, w_a_in, w_a_in, w_a_conv, prev)


def _kv_proj_kernel(x_ref, wk_ref, wv_ref, k32_ref, v32_ref, k16_ref, v16_ref, *, head_dim):
    x = x_ref[...]
    tm, heads = k32_ref.shape[:2]
    for w_ref, o32_ref, o16_ref in ((wk_ref, k32_ref, k16_ref), (wv_ref, v32_ref, v16_ref)):
        y = jnp.dot(x, w_ref[...], preferred_element_type=F32)
        o16_ref[...] = y.astype(BF16)
        rows = o32_ref.reshape(tm * heads, head_dim)
        for h in range(heads):
            rows[pl.ds(h, tm, stride=heads), :] = y[:, h * head_dim:(h + 1) * head_dim]


def kv_proj(x16, w_kv16, row_start, n_rows, n_heads, head_dim):
    d = x16.shape[1]
    tm = _pick_tile(math.gcd(n_rows, row_start) if row_start else n_rows, (512, 256, 128))
    heads_per_block = SUBLANES
    tn = heads_per_block * head_dim
    n_blocks = d // tn
    m0 = row_start // tm
    spec32 = pl.BlockSpec((tm, heads_per_block, head_dim), lambda n, m: (m, n, 0))
    spec16 = pl.BlockSpec((tm, tn), lambda n, m: (m, n))
    return pl.pallas_call(
        functools.partial(_kv_proj_kernel, head_dim=head_dim),
        grid=(n_blocks, n_rows // tm),
        in_specs=[
            pl.BlockSpec((tm, d), lambda n, m: (m + m0, 0)),
            pl.BlockSpec((d, tn), lambda n, m: (0, n)),
            pl.BlockSpec((d, tn), lambda n, m: (0, n + n_blocks)),
        ],
        out_specs=[spec32, spec32, spec16, spec16],
        out_shape=[jax.ShapeDtypeStruct((n_rows, n_heads, head_dim), F32)] * 2
                  + [jax.ShapeDtypeStruct((n_rows, d), BF16)] * 2,
        compiler_params=_params(),
        name="kv_proj",
    )(x16, w_kv16, w_kv16)


def _q_proj_kernel(x_ref, w_ref, o_ref, w16_ref, *, scale):
    _cast_once(pl.program_id(1), ((w_ref, w16_ref),))
    q = jnp.dot(x_ref[...], w16_ref[...], preferred_element_type=F32)
    o_ref[...] = (q * scale).astype(o_ref.dtype)


def q_proj(x16, w_q, layer, scale):
    m_rows, d = x16.shape
    tm = _pick_tile(m_rows, (1056, 1024, 512, 256, 128))
    tn = _pick_tile(d, (512, 256, 128))
    return pl.pallas_call(
        functools.partial(_q_proj_kernel, scale=scale),
        grid=(d // tn, m_rows // tm),
        in_specs=[
            pl.BlockSpec((tm, d), lambda n, m: (m, 0)),
            pl.BlockSpec((None, d, tn), lambda n, m: (layer, 0, n)),
        ],
        out_specs=pl.BlockSpec((tm, tn), lambda n, m: (m, n)),
        out_shape=jax.ShapeDtypeStruct((m_rows, d), BF16),
        scratch_shapes=[pltpu.VMEM((d, tn), BF16)],
        compiler_params=_params(),
        name="q_proj",
    )(x16, w_q)


def _suffix_matrix(n):
    j = lax.broadcasted_iota(jnp.int32, (2 * n, n), 0) % n
    s = lax.broadcasted_iota(jnp.int32, (2 * n, n), 1)
    return (j > s).astype(BF16)


def _log2_keep(zn, causal):
    lk = jnp.minimum(zn, 0.0) - jnp.log2(1.0 + jnp.exp2(-jnp.abs(zn)))
    return lk if causal is None else jnp.where(causal, lk, 0.0)


def _stick_weights(zn, lk, suffix, carry):
    hi = lk.astype(BF16)
    lo = (lk - hi.astype(F32)).astype(BF16)
    after = jnp.dot(jnp.concatenate([hi, lo], axis=1), suffix, preferred_element_type=F32)
    w = jnp.exp2((lk - zn) + after + carry)
    return w, carry + jnp.sum(lk, axis=-1, keepdims=True)


_NT = (((1,), (1,)), ((), ()))


def _prompt_tile(qi, q_ref, k_ref, v_ref, suffix_ref, o_ref, *, tq, sub):
    q = q_ref[...]
    suffix = suffix_ref[...]
    head_dim = q.shape[1]
    n_sub = tq // sub

    def visit(key_start, carry, acc, causal):
        k = k_ref[pl.ds(key_start, tq), :]
        v = v_ref[pl.ds(key_start, tq), :]
        zn = lax.dot_general(q, k, _NT, preferred_element_type=F32)
        lk = _log2_keep(zn, causal)
        ws = [None] * n_sub
        for s in reversed(range(n_sub)):
            cols = slice(s * sub, (s + 1) * sub)
            ws[s], carry = _stick_weights(zn[:, cols], lk[:, cols], suffix, carry)
        w = jnp.concatenate(ws, axis=1)
        if causal is not None:
            w = jnp.where(causal, w, 0.0)
        return carry, acc + jnp.dot(w.astype(BF16), v, preferred_element_type=F32)

    own = (lax.broadcasted_iota(jnp.int32, (tq, tq), 1) < lax.broadcasted_iota(jnp.int32, (tq, tq), 0))
    carry, acc = visit(pl.multiple_of(qi * tq, tq), jnp.zeros((tq, 1), F32),
                       jnp.zeros((tq, head_dim), F32), own)

    def any_alive(carry):
        return jnp.max(carry) > ALL_WEIGHTS_ZERO_LOG2

    def cond(loop):
        i, alive, _, _ = loop
        return jnp.logical_and(i < qi, alive)

    def body(loop):
        i, _, carry, acc = loop
        carry, acc = visit(pl.multiple_of((qi - 1 - i) * tq, tq), carry, acc, None)
        return i + 1, any_alive(carry), carry, acc

    _, _, _, acc = lax.while_loop(cond, body, (0, any_alive(carry), carry, acc))
    o_ref[...] = acc.astype(o_ref.dtype)


def _sample_step(t, n_steps, last, pt_ref, q_ref, knew_ref, vnew_ref, ck_ref, cv_ref, suffix_ref, o_ref,
                 kbuf, vbuf, sem, acc_ref, carry_ref, kt_ref, *, pages_per_step, n_heads, dec_seq):
    c = t % n_steps
    slot = t % 2
    head_dim = q_ref.shape[2]
    half_heads = n_heads // 2
    page = kbuf.shape[3]
    rows_per_pair = 2 * dec_seq
    suffix = suffix_ref[...]

    def page_copies(step, dst_slot):
        seq, grp = step // n_steps, step % n_steps
        copies = []
        for i in range(pages_per_step):
            pid = pt_ref[seq, (n_steps - 1 - grp) * pages_per_step + (pages_per_step - 1 - i)]
            keys = pl.ds(pl.multiple_of(pid * page, page), page)
            for half in range(2):
                heads = pl.ds(half * half_heads, half_heads)
                for src, dst in ((ck_ref, kbuf), (cv_ref, vbuf)):
                    copies.append(pltpu.make_async_copy(
                        src.at[keys, heads, :], dst.at[dst_slot, i, half], sem.at[dst_slot]))
        return copies

    @pl.when(t == 0)
    def _():
        for cp in page_copies(t, slot):
            cp.start()

    @pl.when(t < last)
    def _():
        for cp in page_copies(t + 1, 1 - slot):
            cp.start()

    def visit(k_head, v_head, causal):
        zs = []
        for hp in range(n_heads // 2):
            lhs = q_ref[0, hp * rows_per_pair:(hp + 1) * rows_per_pair, :]
            for half in range(2):
                h = 2 * hp + half
                kt_ref[h] = k_head(h).T
                z_h = jnp.dot(lhs, kt_ref[h].astype(BF16), preferred_element_type=F32)
                zs.append(z_h[half * dec_seq:(half + 1) * dec_seq])
        zn = jnp.concatenate(zs, axis=0)
        w, carry = _stick_weights(zn, _log2_keep(zn, causal), suffix, carry_ref[...])
        if causal is not None:
            w = jnp.where(causal, w, 0.0)
        carry_ref[...] = carry
        w16 = w.astype(BF16)
        outs = []
        for hp in range(n_heads // 2):
            lhs = w16[hp * rows_per_pair:(hp + 1) * rows_per_pair, :]
            for half in range(2):
                o_h = jnp.dot(lhs, v_head(2 * hp + half).astype(BF16), preferred_element_type=F32)
                outs.append(o_h[half * dec_seq:(half + 1) * dec_seq])
        acc_ref[...] += jnp.concatenate(outs, axis=0)

    def lane_slab(ref):
        return lambda h: ref[0, :, h * head_dim:(h + 1) * head_dim].astype(F32)

    def strided_rows(buf, i):
        def get(h):
            rows = buf.at[slot, i, h // half_heads].reshape(page * half_heads, head_dim)
            return rows[pl.ds(h % half_heads, page, stride=half_heads), :]
        return get

    @pl.when(c == 0)
    def _():
        acc_ref[...] = jnp.zeros_like(acc_ref)
        carry_ref[...] = jnp.zeros_like(carry_ref)
        n_new = knew_ref.shape[1]
        row = lax.broadcasted_iota(jnp.int32, (n_heads * dec_seq, n_new), 0)
        col = lax.broadcasted_iota(jnp.int32, (n_heads * dec_seq, n_new), 1)
        visit(lane_slab(knew_ref), lane_slab(vnew_ref), col < (row % dec_seq))

    for cp in page_copies(t, slot):
        cp.wait()

    for i in range(pages_per_step):
        visit(strided_rows(kbuf, i), strided_rows(vbuf, i), None)

    @pl.when(c == n_steps - 1)
    def _():
        o_ref[0] = acc_ref[...].astype(o_ref.dtype)


def _sb_attention_kernel(pt_ref, pq_ref, pk_ref, pv_ref, psuffix_ref, sq_ref, knew_ref, vnew_ref,
                         ck_ref, cv_ref, ssuffix_ref, po_ref, so_ref, *scratch,
                         n_prompt, n_q, tq, sub, n_sample, n_steps, pages_per_step, n_heads, dec_seq):
    s = pl.program_id(0)

    def sample():
        _sample_step(s, n_steps, n_sample - 1, pt_ref, sq_ref, knew_ref, vnew_ref, ck_ref, cv_ref,
                     ssuffix_ref, so_ref, *scratch, pages_per_step=pages_per_step,
                     n_heads=n_heads, dec_seq=dec_seq)

    def prompt():
        _prompt_tile(s % n_q, pq_ref, pk_ref, pv_ref, psuffix_ref, po_ref, tq=tq, sub=sub)

    for count, part in ((n_sample, sample), (n_prompt, prompt)):
        if count == max(n_sample, n_prompt):
            part()
        else:
            pl.when(s < count)(part)


def sb_attention(q16, k16_p, v16_p, n_seq, seq_len, q_hm, knew_pad, vnew_pad, cache_k, cache_v,
                 page_table, dec_seq):
    n_s, rows, head_dim = q_hm.shape
    n_phys, page, n_heads, _ = cache_k.shape
    n_pages = page_table.shape[1]
    pages_per_step = _pick_tile(n_pages, (8, 4, 2, 1))
    n_steps = n_pages // pages_per_step
    assert n_heads % (2 * SUBLANES) == 0
    half_heads = n_heads // 2
    n_sample = n_s * n_steps

    tq = _pick_tile(seq_len, (512, 256, 128))
    sub = min(256, tq)
    n_q = seq_len // tq
    n_prompt = n_seq * n_heads * n_q

    def prompt_tile(s):
        s = jnp.minimum(s, n_prompt - 1)
        return s // (n_heads * n_q), (s // n_q) % n_heads, s % n_q

    def q_index(s, pt):
        b, h, i = prompt_tile(s)
        return b * n_q + i, h

    def kv_index(s, pt):
        b, h, _ = prompt_tile(s)
        return b, h

    def seq_index(s, pt):
        return jnp.minimum(s, n_sample - 1) // n_steps, 0, 0

    kern = functools.partial(_sb_attention_kernel, n_prompt=n_prompt, n_q=n_q, tq=tq, sub=sub,
                             n_sample=n_sample, n_steps=n_steps, pages_per_step=pages_per_step,
                             n_heads=n_heads, dec_seq=dec_seq)
    page_buffers = pltpu.VMEM((2, pages_per_step, 2, page, half_heads, head_dim), F32)
    grid_spec = pltpu.PrefetchScalarGridSpec(
        num_scalar_prefetch=1,
        grid=(max(n_prompt, n_sample),),
        in_specs=[
            pl.BlockSpec((tq, head_dim), q_index),
            pl.BlockSpec((seq_len, head_dim), kv_index),
            pl.BlockSpec((seq_len, head_dim), kv_index),
            pl.BlockSpec((2 * sub, sub), lambda s, pt: (0, 0)),
            pl.BlockSpec((1, rows, head_dim), seq_index),
            pl.BlockSpec((1, page, n_heads * head_dim), seq_index),
            pl.BlockSpec((1, page, n_heads * head_dim), seq_index),
            pl.BlockSpec(memory_space=pl.ANY),
            pl.BlockSpec(memory_space=pl.ANY),
            pl.BlockSpec((2 * page, page), lambda s, pt: (0, 0)),
        ],
        out_specs=[pl.BlockSpec((tq, head_dim), q_index),
                   pl.BlockSpec((1, rows, head_dim), seq_index)],
        scratch_shapes=[page_buffers, page_buffers, pltpu.SemaphoreType.DMA((2,)),
                        pltpu.VMEM((rows, head_dim), F32), pltpu.VMEM((rows, 1), F32),
                        pltpu.VMEM((n_heads, head_dim, page), F32)],
    )
    ck = cache_k.reshape(n_phys * page, n_heads, head_dim)
    cv = cache_v.reshape(n_phys * page, n_heads, head_dim)
    return pl.pallas_call(
        kern,
        grid_spec=grid_spec,
        out_shape=[jax.ShapeDtypeStruct(k16_p.shape, BF16),
                   jax.ShapeDtypeStruct((n_s, rows, head_dim), BF16)],
        compiler_params=pltpu.CompilerParams(vmem_limit_bytes=VMEM_LIMIT_BYTES,
                                             dimension_semantics=("arbitrary",)),
        name="sb_attention",
    )(page_table, q16, k16_p, v16_p, _suffix_matrix(sub), q_hm, knew_pad, vnew_pad, ck, cv,
      _suffix_matrix(page))


def kernel(x_prompt, x_sample, state_conv, cache_k, cache_v, page_table, ln_g, ln_b, w_ffn_in,
           w_ffn_out, w_a_in, w_a_conv, w_a_out, w_kv, w_q, w_o):
    n_p, seq, d = x_prompt.shape
    n_s, dec_seq, _ = x_sample.shape
    depth = ln_g.shape[0]
    page, n_heads, head_dim = cache_k.shape[1:]
    rows_p, rows_s = n_p * seq, n_s * dec_seq
    alpha = (2.0 * depth) ** 0.25
    assert w_a_in.shape[0] == 1 and depth == 2, "one short-conv layer followed by one attention layer"

    x32 = jnp.concatenate([x_prompt.reshape(rows_p, d), x_sample.reshape(rows_s, d)], axis=0)
    x16 = x32.astype(BF16)

    def ffn(x32, x16, layer, half, ln_idx, row_splits=None):
        h = swiglu_in(x16, w_ffn_in, layer, half)
        w_out16 = cast_weight(w_ffn_out, (layer, half))
        norm = functools.partial(proj_res_ln, h, w_out16, x32, ln_g[layer, ln_idx], ln_b[layer, ln_idx],
                                 alpha=alpha, coef=0.5)
        if row_splits is None:
            return norm()
        return [norm(row_start=r0, n_rows=n, out_dtypes=(F32,))[0] for r0, n in row_splits]

    x32, x16 = ffn(x32, x16, 0, 0, 0)
    g, conv_prompt = mixer_in_long(x16, w_a_in, w_a_conv, 0, n_p, seq)
    g_s, conv_sample = mixer_in_short(x16, w_a_in, w_a_conv, state_conv, 0, rows_p, n_s, dec_seq)
    x32, x16 = proj_res_ln(jnp.concatenate([g, g_s], axis=0), cast_weight(w_a_out, (0,)), x32,
                           ln_g[0, 1], ln_b[0, 1], alpha=alpha, coef=1.0)
    x32, x16 = ffn(x32, x16, 0, 1, 2)

    w_kv16 = w_kv.astype(BF16)
    k_prompt, v_prompt, k16_p, v16_p = kv_proj(x16, w_kv16, 0, rows_p, n_heads, head_dim)
    k_sample, v_sample, k16_s, v16_s = kv_proj(x16, w_kv16, rows_p, rows_s, n_heads, head_dim)

    x32, x16 = ffn(x32, x16, 1, 0, 0)
    q16 = q_proj(x16, w_q, 0, -math.log2(math.e) * head_dim ** -0.5)
    q_hm = (q16[rows_p:].reshape(n_s, dec_seq, n_heads, head_dim)
            .transpose(0, 2, 1, 3).reshape(n_s, n_heads * dec_seq, head_dim))
    pad = ((0, 0), (0, page - dec_seq), (0, 0))
    o, o_hm = sb_attention(q16, k16_p, v16_p, n_p, seq, q_hm,
                           jnp.pad(k16_s.reshape(n_s, dec_seq, d), pad),
                           jnp.pad(v16_s.reshape(n_s, dec_seq, d), pad),
                           cache_k, cache_v, page_table, dec_seq)
    o_s = o_hm.reshape(n_s, n_heads, dec_seq, head_dim).transpose(0, 2, 1, 3).reshape(rows_s, d)

    x32, x16 = proj_res_ln(jnp.concatenate([o, o_s], axis=0), cast_weight(w_o, (0,)), x32,
                           ln_g[1, 1], ln_b[1, 1], alpha=alpha, coef=1.0)
    y_prompt, y_sample = ffn(x32, x16, 1, 1, 2, row_splits=((0, rows_p), (rows_p, rows_s)))

    return (y_prompt.reshape(n_p, seq, d), y_sample.reshape(n_s, dec_seq, d),
            conv_prompt[None], conv_sample[None],
            k_prompt.reshape(n_p, seq, n_heads, head_dim), v_prompt.reshape(n_p, seq, n_heads, head_dim),
            k_sample.reshape(n_s, dec_seq, n_heads, head_dim),
            v_sample.reshape(n_s, dec_seq, n_heads, head_dim))
```

```python
import functools
import math

import jax
import jax.numpy as jnp
from jax import lax
from jax.experimental import pallas as pl
from jax.experimental.pallas import tpu as pltpu

F32 = jnp.float32
BF16 = jnp.bfloat16

LN_EPS = 1e-5
CONV_W = 3
SUBLANES = 8
VMEM_LIMIT_BYTES = 56 * 1024 * 1024
ALL_WEIGHTS_ZERO_LOG2 = -160.0


def _params():
    return pltpu.CompilerParams(vmem_limit_bytes=VMEM_LIMIT_BYTES)


def _pick_tile(total, candidates):
    for c in candidates:
        if total % c == 0:
            return c
    return total


def _cast_once(step, pairs):
    @pl.when(step == 0)
    def _():
        for src, dst in pairs:
            dst[...] = src[...].astype(BF16)


def _cast_kernel(w_ref, o_ref):
    o_ref[...] = w_ref[...].astype(BF16)


def cast_weight(w_stack, lead):
    k_dim, n_dim = w_stack.shape[-2:]
    tk = _pick_tile(k_dim, (512, 256, 128))
    nones = (None,) * len(lead)
    return pl.pallas_call(
        _cast_kernel,
        grid=(k_dim // tk,),
        in_specs=[pl.BlockSpec(nones + (tk, n_dim), lambda k: lead + (k, 0))],
        out_specs=pl.BlockSpec((tk, n_dim), lambda k: (k, 0)),
        out_shape=jax.ShapeDtypeStruct((k_dim, n_dim), BF16),
        compiler_params=_params(),
        name="cast_weight",
    )(w_stack)


def _stack_rows_kernel(top_ref, bottom_ref, x32_ref, x16_ref, *, n_top_tiles):
    x = jnp.where(pl.program_id(0) < n_top_tiles, top_ref[...], bottom_ref[...])
    x32_ref[...] = x
    x16_ref[...] = x.astype(BF16)


def stack_rows(top, bottom):
    (rows_top, d), tm = top.shape, bottom.shape[0]
    assert rows_top % tm == 0 and tm % (2 * SUBLANES) == 0
    n_top_tiles = rows_top // tm
    out_spec = pl.BlockSpec((tm, d), lambda m: (m, 0))
    return pl.pallas_call(
        functools.partial(_stack_rows_kernel, n_top_tiles=n_top_tiles),
        grid=(n_top_tiles + 1,),
        in_specs=[pl.BlockSpec((tm, d), lambda m: (jnp.minimum(m, n_top_tiles - 1), 0)),
                  pl.BlockSpec((tm, d), lambda m: (0, 0))],
        out_specs=[out_spec, out_spec],
        out_shape=[jax.ShapeDtypeStruct((rows_top + tm, d), F32),
                   jax.ShapeDtypeStruct((rows_top + tm, d), BF16)],
        compiler_params=_params(),
        name="stack_rows",
    )(top, bottom)


def _swiglu_in_kernel(x_ref, wa_ref, wg_ref, o_ref, wa16_ref, wg16_ref):
    _cast_once(pl.program_id(1), ((wa_ref, wa16_ref), (wg_ref, wg16_ref)))
    x = x_ref[...]
    a = jnp.dot(x, wa16_ref[...], preferred_element_type=F32)
    g = jnp.dot(x, wg16_ref[...], preferred_element_type=F32)
    o_ref[...] = (a * jax.nn.sigmoid(a) * g).astype(o_ref.dtype)


def swiglu_in(x16, w_ffn_in, layer, half):
    m_rows, d = x16.shape
    f = w_ffn_in.shape[-1] // 2
    tm = _pick_tile(m_rows, (1056, 1024, 512, 256, 128))
    tn = _pick_tile(f, (512, 256, 128))
    n_blocks = f // tn
    return pl.pallas_call(
        _swiglu_in_kernel,
        grid=(n_blocks, m_rows // tm),
        in_specs=[
            pl.BlockSpec((tm, d), lambda n, m: (m, 0)),
            pl.BlockSpec((None, None, d, tn), lambda n, m: (layer, half, 0, n)),
            pl.BlockSpec((None, None, d, tn), lambda n, m: (layer, half, 0, n + n_blocks)),
        ],
        out_specs=pl.BlockSpec((tm, tn), lambda n, m: (m, n)),
        out_shape=jax.ShapeDtypeStruct((m_rows, f), BF16),
        scratch_shapes=[pltpu.VMEM((d, tn), BF16)] * 2,
        compiler_params=_params(),
        name="swiglu_in",
    )(x16, w_ffn_in, w_ffn_in)


def _proj_res_ln_kernel(a_ref, w_ref, res_ref, gain_ref, bias_ref, *out_refs, alpha, coef, chunk):
    for r in range(a_ref.shape[0] // chunk):
        rows = slice(r * chunk, (r + 1) * chunk)
        y = alpha * res_ref[rows, :] + coef * jnp.dot(a_ref[rows, :], w_ref[...],
                                                      preferred_element_type=F32)
        mu = jnp.mean(y, axis=-1, keepdims=True)
        dev = y - mu
        var = jnp.mean(dev * dev, axis=-1, keepdims=True)
        out = dev * lax.rsqrt(var + LN_EPS) * gain_ref[...] + bias_ref[...]
        for o_ref in out_refs:
            o_ref[rows, :] = out.astype(o_ref.dtype)


def proj_res_ln(a16, w16, res32, gain, bias, *, alpha, coef, row_start=0, n_rows=None,
                out_dtypes=(F32, BF16)):
    k_dim = a16.shape[1]
    d = w16.shape[1]
    n_rows = a16.shape[0] if n_rows is None else n_rows
    tiles = (528, 512, 256, 128) if k_dim <= d else (384, 256, 128)
    tm = _pick_tile(math.gcd(n_rows, row_start) if row_start else n_rows, tiles)
    chunk = _pick_tile(tm, (192, 176, 128))
    m0 = row_start // tm
    kern = functools.partial(_proj_res_ln_kernel, alpha=alpha, coef=coef, chunk=chunk)
    return pl.pallas_call(
        kern,
        grid=(n_rows // tm,),
        in_specs=[
            pl.BlockSpec((tm, k_dim), lambda m: (m + m0, 0)),
            pl.BlockSpec((k_dim, d), lambda m: (0, 0), pipeline_mode=pl.Buffered(1)),
            pl.BlockSpec((tm, d), lambda m: (m + m0, 0)),
            pl.BlockSpec((1, d), lambda m: (0, 0)),
            pl.BlockSpec((1, d), lambda m: (0, 0)),
        ],
        out_specs=[pl.BlockSpec((tm, d), lambda m: (m, 0)) for _ in out_dtypes],
        out_shape=[jax.ShapeDtypeStruct((n_rows, d), dt) for dt in out_dtypes],
        compiler_params=_params(),
        name="proj_res_ln",
    )(a16, w16, res32, gain.reshape(1, d), bias.reshape(1, d))


def _conv_taps(wconv_ref):
    wc = wconv_ref[...]
    return wc[0:1], wc[1:2], wc[2:3]


def _gates(x_ref, w16_refs):
    x = x_ref[...]
    b_gate, c_gate, h = (jnp.dot(x, w[...], preferred_element_type=F32) for w in w16_refs)
    return b_gate, c_gate * h


def _mixer_in_long_kernel(x_ref, wb_ref, wc_ref, wh_ref, wconv_ref, g_ref, st_ref,
                          wb16_ref, wc16_ref, wh16_ref, hist_ref, *, tiles_per_seq, tm):
    m = pl.program_id(1)
    w16 = (wb16_ref, wc16_ref, wh16_ref)
    _cast_once(m, zip((wb_ref, wc_ref, wh_ref), w16))
    b_gate, u = _gates(x_ref, w16)

    @pl.when(m % tiles_per_seq == 0)
    def _():
        hist_ref[0:SUBLANES, :] = jnp.zeros((SUBLANES, hist_ref.shape[1]), F32)

    hist_ref[SUBLANES:SUBLANES + tm, :] = u
    w0, w1, w2 = _conv_taps(wconv_ref)
    conv = (w0 * hist_ref[SUBLANES - 2:SUBLANES - 2 + tm, :]
            + w1 * hist_ref[SUBLANES - 1:SUBLANES - 1 + tm, :]
            + w2 * u)
    g_ref[...] = (b_gate * conv).astype(g_ref.dtype)
    st_ref[0] = u[tm - (CONV_W - 1):tm, :]
    hist_ref[0:SUBLANES, :] = u[tm - SUBLANES:tm, :]


def _mixer_w_specs(d, tn, n_blocks, index):
    return [pl.BlockSpec((None, d, tn), index(j * n_blocks)) for j in range(3)]


def mixer_in_long(x16, w_a_in, w_a_conv, layer, n_seq, seq_len):
    d = x16.shape[1]
    tm = _pick_tile(seq_len, (512, 256, 128))
    tn = _pick_tile(d, (512, 256, 128))
    n_blocks = d // tn
    tiles_per_seq = seq_len // tm
    kern = functools.partial(_mixer_in_long_kernel, tiles_per_seq=tiles_per_seq, tm=tm)
    return pl.pallas_call(
        kern,
        grid=(n_blocks, n_seq * tiles_per_seq),
        in_specs=[pl.BlockSpec((tm, d), lambda n, m: (m, 0))]
        + _mixer_w_specs(d, tn, n_blocks, lambda off: (lambda n, m: (layer, 0, n + off)))
        + [pl.BlockSpec((None, CONV_W, tn), lambda n, m: (layer, 0, n))],
        out_specs=[
            pl.BlockSpec((tm, tn), lambda n, m: (m, n)),
            pl.BlockSpec((1, CONV_W - 1, tn), lambda n, m: (m // tiles_per_seq, 0, n)),
        ],
        out_shape=[jax.ShapeDtypeStruct((n_seq * seq_len, d), BF16),
                   jax.ShapeDtypeStruct((n_seq, CONV_W - 1, d), F32)],
        scratch_shapes=[pltpu.VMEM((d, tn), BF16)] * 3 + [pltpu.VMEM((tm + SUBLANES, tn), F32)],
        compiler_params=_params(),
        name="mixer_in_long",
    )(x16, w_a_in, w_a_in, w_a_in, w_a_conv)


def _mixer_in_short_kernel(x_ref, wb_ref, wc_ref, wh_ref, wconv_ref, prev_ref, g_ref, st_ref,
                           *, seq_len):
    rows, tn = x_ref.shape[0], wb_ref.shape[1]
    n_seq = rows // seq_len
    x = x_ref[...]
    b_gate, c_gate, h = (jnp.dot(x, w[...].astype(BF16), preferred_element_type=F32)
                         for w in (wb_ref, wc_ref, wh_ref))
    u3 = (c_gate * h).reshape(n_seq, seq_len, tn)
    prev = prev_ref[...]
    p0, p1 = prev[:, 0:1, :], prev[:, 1:2, :]
    pos = lax.broadcasted_iota(jnp.int32, u3.shape, 1)
    u1 = jnp.where(pos == 0, p1, pltpu.roll(u3, 1, axis=1))
    u2 = jnp.where(pos == 0, p0, jnp.where(pos == 1, p1, pltpu.roll(u3, 2, axis=1)))
    w0, w1, w2 = _conv_taps(wconv_ref)
    conv = w0 * u2 + w1 * u1 + w2 * u3
    g_ref[...] = (b_gate * conv.reshape(rows, tn)).astype(g_ref.dtype)
    st_ref[...] = u3[:, seq_len - (CONV_W - 1):seq_len, :]


def mixer_in_short(x16, w_a_in, w_a_conv, prev, layer, row_start, n_seq, seq_len):
    d = x16.shape[1]
    rows = n_seq * seq_len
    assert seq_len == SUBLANES and row_start % rows == 0
    tn = _pick_tile(d, (512, 256, 128))
    n_blocks = d // tn
    kern = functools.partial(_mixer_in_short_kernel, seq_len=seq_len)
    return pl.pallas_call(
        kern,
        grid=(n_blocks,),
        in_specs=[pl.BlockSpec((rows, d), lambda n: (row_start // rows, 0))]
        + _mixer_w_specs(d, tn, n_blocks, lambda off: (lambda n: (layer, 0, n + off)))
        + [pl.BlockSpec((None, CONV_W, tn), lambda n: (layer, 0, n)),
           pl.BlockSpec((None, n_seq, CONV_W - 1, tn), lambda n: (layer, 0, 0, n))],
        out_specs=[
            pl.BlockSpec((rows, tn), lambda n: (0, n)),
            pl.BlockSpec((n_seq, CONV_W - 1, tn), lambda n: (0, 0, n)),
        ],
        out_shape=[jax.ShapeDtypeStruct((rows, d), BF16),
                   jax.ShapeDtypeStruct((n_seq, CONV_W - 1, d), F32)],
        compiler_params=_params(),
        name="mixer_in_short",
    )(x16, w_a_in, w_a_in, w_a_in, w_a_conv, prev)


def _kv_proj_kernel(x_ref, wk_ref, wv_ref, k32_ref, v32_ref, k16_ref, v16_ref, *, head_dim):
    x = x_ref[...]
    tm, heads = k32_ref.shape[:2]
    for w_ref, o32_ref, o16_ref in ((wk_ref, k32_ref, k16_ref), (wv_ref, v32_ref, v16_ref)):
        y = jnp.dot(x, w_ref[...], preferred_element_type=F32)
        o16_ref[...] = y.astype(BF16)
        rows = o32_ref.reshape(tm * heads, head_dim)
        for h in range(heads):
            rows[pl.ds(h, tm, stride=heads), :] = y[:, h * head_dim:(h + 1) * head_dim]


def kv_proj(x16, w_kv16, row_start, n_rows, n_heads, head_dim):
    d = x16.shape[1]
    tm = _pick_tile(math.gcd(n_rows, row_start) if row_start else n_rows, (512, 256, 128))
    heads_per_block = SUBLANES
    tn = heads_per_block * head_dim
    n_blocks = d // tn
    m0 = row_start // tm
    spec32 = pl.BlockSpec((tm, heads_per_block, head_dim), lambda n, m: (m, n, 0))
    spec16 = pl.BlockSpec((tm, tn), lambda n, m: (m, n))
    return pl.pallas_call(
        functools.partial(_kv_proj_kernel, head_dim=head_dim),
        grid=(n_blocks, n_rows // tm),
        in_specs=[
            pl.BlockSpec((tm, d), lambda n, m: (m + m0, 0)),
            pl.BlockSpec((d, tn), lambda n, m: (0, n)),
            pl.BlockSpec((d, tn), lambda n, m: (0, n + n_blocks)),
        ],
        out_specs=[spec32, spec32, spec16, spec16],
        out_shape=[jax.ShapeDtypeStruct((n_rows, n_heads, head_dim), F32)] * 2
                  + [jax.ShapeDtypeStruct((n_rows, d), BF16)] * 2,
        compiler_params=_params(),
        name="kv_proj",
    )(x16, w_kv16, w_kv16)


def _q_proj_kernel(x_ref, w_ref, o_ref, w16_ref, *, scale):
    _cast_once(pl.program_id(1), ((w_ref, w16_ref),))
    q = jnp.dot(x_ref[...], w16_ref[...], preferred_element_type=F32)
    o_ref[...] = (q * scale).astype(o_ref.dtype)


def q_proj(x16, w_q, layer, scale):
    m_rows, d = x16.shape
    tm = _pick_tile(m_rows, (1056, 1024, 512, 256, 128))
    tn = _pick_tile(d, (512, 256, 128))
    return pl.pallas_call(
        functools.partial(_q_proj_kernel, scale=scale),
        grid=(d // tn, m_rows // tm),
        in_specs=[
            pl.BlockSpec((tm, d), lambda n, m: (m, 0)),
            pl.BlockSpec((None, d, tn), lambda n, m: (layer, 0, n)),
        ],
        out_specs=pl.BlockSpec((tm, tn), lambda n, m: (m, n)),
        out_shape=jax.ShapeDtypeStruct((m_rows, d), BF16),
        scratch_shapes=[pltpu.VMEM((d, tn), BF16)],
        compiler_params=_params(),
        name="q_proj",
    )(x16, w_q)


def _suffix_matrix(n):
    j = lax.broadcasted_iota(jnp.int32, (2 * n, n), 0) % n
    s = lax.broadcasted_iota(jnp.int32, (2 * n, n), 1)
    return (j > s).astype(BF16)


def _log2_keep(zn, causal):
    lk = jnp.minimum(zn, 0.0) - jnp.log2(1.0 + jnp.exp2(-jnp.abs(zn)))
    return lk if causal is None else jnp.where(causal, lk, 0.0)


def _stick_weights(zn, lk, suffix, carry):
    hi = lk.astype(BF16)
    lo = (lk - hi.astype(F32)).astype(BF16)
    after = jnp.dot(jnp.concatenate([hi, lo], axis=1), suffix, preferred_element_type=F32)
    w = jnp.exp2((lk - zn) + after + carry)
    return w, carry + jnp.sum(lk, axis=-1, keepdims=True)


_NT = (((1,), (1,)), ((), ()))


def _prompt_tile(qi, q_ref, k_ref, v_ref, suffix_ref, o_ref, *, tq, sub):
    q = q_ref[...]
    suffix = suffix_ref[...]
    head_dim = q.shape[1]
    n_sub = tq // sub

    def visit(key_start, n_blocks, carry, acc, causal):
        k = k_ref[pl.ds(key_start, n_blocks * sub), :]
        v = v_ref[pl.ds(key_start, n_blocks * sub), :]
        zn = lax.dot_general(q, k, _NT, preferred_element_type=F32)
        lk = _log2_keep(zn, causal)
        ws = [None] * n_blocks
        for s in reversed(range(n_blocks)):
            cols = slice(s * sub, (s + 1) * sub)
            ws[s], carry = _stick_weights(zn[:, cols], lk[:, cols], suffix, carry)
        w = jnp.concatenate(ws, axis=1) if n_blocks > 1 else ws[0]
        if causal is not None:
            w = jnp.where(causal, w, 0.0)
        return carry, acc + jnp.dot(w.astype(BF16), v, preferred_element_type=F32)

    own = (lax.broadcasted_iota(jnp.int32, (tq, tq), 1) < lax.broadcasted_iota(jnp.int32, (tq, tq), 0))
    carry, acc = visit(pl.multiple_of(qi * tq, tq), n_sub, jnp.zeros((tq, 1), F32),
                       jnp.zeros((tq, head_dim), F32), own)

    n_older = qi * n_sub

    def any_alive(carry):
        return jnp.max(carry) > ALL_WEIGHTS_ZERO_LOG2

    def cond(loop):
        i, alive, _, _ = loop
        return jnp.logical_and(i < n_older, alive)

    def body(loop):
        i, _, carry, acc = loop
        carry, acc = visit(pl.multiple_of((n_older - 1 - i) * sub, sub), 1, carry, acc, None)
        return i + 1, any_alive(carry), carry, acc

    _, _, _, acc = lax.while_loop(cond, body, (0, any_alive(carry), carry, acc))
    o_ref[...] = acc.astype(o_ref.dtype)


def _sample_step(t, n_steps, last, pt_ref, q_ref, knew_ref, vnew_ref, ck_ref, cv_ref, suffix_ref, o_ref,
                 kbuf, vbuf, sem, acc_ref, carry_ref, kt_ref, *, pages_per_step, n_heads, dec_seq):
    c = t % n_steps
    slot = t % 2
    head_dim = q_ref.shape[2]
    half_heads = n_heads // 2
    page = kbuf.shape[3]
    rows_per_pair = 2 * dec_seq
    suffix = suffix_ref[...]

    def page_copies(step, dst_slot):
        seq, grp = step // n_steps, step % n_steps
        copies = []
        for i in range(pages_per_step):
            pid = pt_ref[seq, (n_steps - 1 - grp) * pages_per_step + (pages_per_step - 1 - i)]
            keys = pl.ds(pl.multiple_of(pid * page, page), page)
            for half in range(2):
                heads = pl.ds(half * half_heads, half_heads)
                for src, dst in ((ck_ref, kbuf), (cv_ref, vbuf)):
                    copies.append(pltpu.make_async_copy(
                        src.at[keys, heads, :], dst.at[dst_slot, i, half], sem.at[dst_slot]))
        return copies

    @pl.when(t == 0)
    def _():
        for cp in page_copies(t, slot):
            cp.start()

    @pl.when(t < last)
    def _():
        for cp in page_copies(t + 1, 1 - slot):
            cp.start()

    def visit(k_head, v_head, causal):
        zs = []
        for hp in range(n_heads // 2):
            lhs = q_ref[0, hp * rows_per_pair:(hp + 1) * rows_per_pair, :]
            for half in range(2):
                h = 2 * hp + half
                kt_ref[h] = k_head(h).T
                z_h = jnp.dot(lhs, kt_ref[h].astype(BF16), preferred_element_type=F32)
                zs.append(z_h[half * dec_seq:(half + 1) * dec_seq])
        zn = jnp.concatenate(zs, axis=0)
        w, carry = _stick_weights(zn, _log2_keep(zn, causal), suffix, carry_ref[...])
        if causal is not None:
            w = jnp.where(causal, w, 0.0)
        carry_ref[...] = carry
        w16 = w.astype(BF16)
        outs = []
        for hp in range(n_heads // 2):
            lhs = w16[hp * rows_per_pair:(hp + 1) * rows_per_pair, :]
            for half in range(2):
                o_h = jnp.dot(lhs, v_head(2 * hp + half).astype(BF16), preferred_element_type=F32)
                outs.append(o_h[half * dec_seq:(half + 1) * dec_seq])
        acc_ref[...] += jnp.concatenate(outs, axis=0)

    def lane_slab(ref):
        return lambda h: ref[0, :, h * head_dim:(h + 1) * head_dim].astype(F32)

    def strided_rows(buf, i):
        def get(h):
            rows = buf.at[slot, i, h // half_heads].reshape(page * half_heads, head_dim)
            return rows[pl.ds(h % half_heads, page, stride=half_heads), :]
        return get

    @pl.when(c == 0)
    def _():
        acc_ref[...] = jnp.zeros_like(acc_ref)
        carry_ref[...] = jnp.zeros_like(carry_ref)
        n_new = knew_ref.shape[1]
        row = lax.broadcasted_iota(jnp.int32, (n_heads * dec_seq, n_new), 0)
        col = lax.broadcasted_iota(jnp.int32, (n_heads * dec_seq, n_new), 1)
        visit(lane_slab(knew_ref), lane_slab(vnew_ref), col < (row % dec_seq))

    for cp in page_copies(t, slot):
        cp.wait()

    for i in range(pages_per_step):
        visit(strided_rows(kbuf, i), strided_rows(vbuf, i), None)

    @pl.when(c == n_steps - 1)
    def _():
        o_ref[0] = acc_ref[...].astype(o_ref.dtype)


def _sb_attention_kernel(pt_ref, pq_ref, pk_ref, pv_ref, psuffix_ref, sq_ref, knew_ref, vnew_ref,
                         ck_ref, cv_ref, ssuffix_ref, po_ref, so_ref, *scratch,
                         n_prompt, n_q, tq, sub, n_sample, n_steps, pages_per_step, n_heads, dec_seq):
    s = pl.program_id(0)

    def sample():
        _sample_step(s, n_steps, n_sample - 1, pt_ref, sq_ref, knew_ref, vnew_ref, ck_ref, cv_ref,
                     ssuffix_ref, so_ref, *scratch, pages_per_step=pages_per_step,
                     n_heads=n_heads, dec_seq=dec_seq)

    def prompt():
        _prompt_tile(s % n_q, pq_ref, pk_ref, pv_ref, psuffix_ref, po_ref, tq=tq, sub=sub)

    for count, part in ((n_sample, sample), (n_prompt, prompt)):
        if count == max(n_sample, n_prompt):
            part()
        else:
            pl.when(s < count)(part)


def sb_attention(q16, k16_p, v16_p, n_seq, seq_len, q_hm, knew_pad, vnew_pad, cache_k, cache_v,
                 page_table, dec_seq):
    n_s, rows, head_dim = q_hm.shape
    n_phys, page, n_heads, _ = cache_k.shape
    n_pages = page_table.shape[1]
    pages_per_step = _pick_tile(n_pages, (8, 4, 2, 1))
    n_steps = n_pages // pages_per_step
    assert n_heads % (2 * SUBLANES) == 0
    half_heads = n_heads // 2
    n_sample = n_s * n_steps

    tq = _pick_tile(seq_len, (512, 256, 128))
    sub = min(256, tq)
    n_q = seq_len // tq
    n_prompt = n_seq * n_heads * n_q

    def prompt_tile(s):
        s = jnp.minimum(s, n_prompt - 1)
        return s // (n_heads * n_q), (s // n_q) % n_heads, s % n_q

    def q_index(s, pt):
        b, h, i = prompt_tile(s)
        return b * n_q + i, h

    def kv_index(s, pt):
        b, h, _ = prompt_tile(s)
        return b, h

    def seq_index(s, pt):
        return jnp.minimum(s, n_sample - 1) // n_steps, 0, 0

    kern = functools.partial(_sb_attention_kernel, n_prompt=n_prompt, n_q=n_q, tq=tq, sub=sub,
                             n_sample=n_sample, n_steps=n_steps, pages_per_step=pages_per_step,
                             n_heads=n_heads, dec_seq=dec_seq)
    page_buffers = pltpu.VMEM((2, pages_per_step, 2, page, half_heads, head_dim), F32)
    grid_spec = pltpu.PrefetchScalarGridSpec(
        num_scalar_prefetch=1,
        grid=(max(n_prompt, n_sample),),
        in_specs=[
            pl.BlockSpec((tq, head_dim), q_index),
            pl.BlockSpec((seq_len, head_dim), kv_index),
            pl.BlockSpec((seq_len, head_dim), kv_index),
            pl.BlockSpec((2 * sub, sub), lambda s, pt: (0, 0)),
            pl.BlockSpec((1, rows, head_dim), seq_index),
            pl.BlockSpec((1, page, n_heads * head_dim), seq_index),
            pl.BlockSpec((1, page, n_heads * head_dim), seq_index),
            pl.BlockSpec(memory_space=pl.ANY),
            pl.BlockSpec(memory_space=pl.ANY),
            pl.BlockSpec((2 * page, page), lambda s, pt: (0, 0)),
        ],
        out_specs=[pl.BlockSpec((tq, head_dim), q_index),
                   pl.BlockSpec((1, rows, head_dim), seq_index)],
        scratch_shapes=[page_buffers, page_buffers, pltpu.SemaphoreType.DMA((2,)),
                        pltpu.VMEM((rows, head_dim), F32), pltpu.VMEM((rows, 1), F32),
                        pltpu.VMEM((n_heads, head_dim, page), F32)],
    )
    ck = cache_k.reshape(n_phys * page, n_heads, head_dim)
    cv = cache_v.reshape(n_phys * page, n_heads, head_dim)
    return pl.pallas_call(
        kern,
        grid_spec=grid_spec,
        out_shape=[jax.ShapeDtypeStruct(k16_p.shape, BF16),
                   jax.ShapeDtypeStruct((n_s, rows, head_dim), BF16)],
        compiler_params=pltpu.CompilerParams(vmem_limit_bytes=VMEM_LIMIT_BYTES,
                                             dimension_semantics=("arbitrary",)),
        name="sb_attention",
    )(page_table, q16, k16_p, v16_p, _suffix_matrix(sub), q_hm, knew_pad, vnew_pad, ck, cv,
      _suffix_matrix(page))


def kernel(x_prompt, x_sample, state_conv, cache_k, cache_v, page_table, ln_g, ln_b, w_ffn_in,
           w_ffn_out, w_a_in, w_a_conv, w_a_out, w_kv, w_q, w_o):
    n_p, seq, d = x_prompt.shape
    n_s, dec_seq, _ = x_sample.shape
    depth = ln_g.shape[0]
    page, n_heads, head_dim = cache_k.shape[1:]
    rows_p, rows_s = n_p * seq, n_s * dec_seq
    alpha = (2.0 * depth) ** 0.25
    assert w_a_in.shape[0] == 1 and depth == 2, "one short-conv layer followed by one attention layer"

    x32, x16 = stack_rows(x_prompt.reshape(rows_p, d), x_sample.reshape(rows_s, d))

    def ffn(x32, x16, layer, half, ln_idx, row_splits=None):
        h = swiglu_in(x16, w_ffn_in, layer, half)
        w_out16 = cast_weight(w_ffn_out, (layer, half))
        norm = functools.partial(proj_res_ln, h, w_out16, x32, ln_g[layer, ln_idx], ln_b[layer, ln_idx],
                                 alpha=alpha, coef=0.5)
        if row_splits is None:
            return norm()
        return [norm(row_start=r0, n_rows=n, out_dtypes=(F32,))[0] for r0, n in row_splits]

    x32, x16 = ffn(x32, x16, 0, 0, 0)
    g, conv_prompt = mixer_in_long(x16, w_a_in, w_a_conv, 0, n_p, seq)
    g_s, conv_sample = mixer_in_short(x16, w_a_in, w_a_conv, state_conv, 0, rows_p, n_s, dec_seq)
    x32, x16 = proj_res_ln(jnp.concatenate([g, g_s], axis=0), cast_weight(w_a_out, (0,)), x32,
                           ln_g[0, 1], ln_b[0, 1], alpha=alpha, coef=1.0)
    x32, x16 = ffn(x32, x16, 0, 1, 2)

    w_kv16 = w_kv.astype(BF16)
    k_prompt, v_prompt, k16_p, v16_p = kv_proj(x16, w_kv16, 0, rows_p, n_heads, head_dim)
    k_sample, v_sample, k16_s, v16_s = kv_proj(x16, w_kv16, rows_p, rows_s, n_heads, head_dim)

    x32, x16 = ffn(x32, x16, 1, 0, 0)
    q16 = q_proj(x16, w_q, 0, -math.log2(math.e) * head_dim ** -0.5)
    q_hm = (q16[rows_p:].reshape(n_s, dec_seq, n_heads, head_dim)
            .transpose(0, 2, 1, 3).reshape(n_s, n_heads * dec_seq, head_dim))
    pad = ((0, 0), (0, page - dec_seq), (0, 0))
    o, o_hm = sb_attention(q16, k16_p, v16_p, n_p, seq, q_hm,
                           jnp.pad(k16_s.reshape(n_s, dec_seq, d), pad),
                           jnp.pad(v16_s.reshape(n_s, dec_seq, d), pad),
                           cache_k, cache_v, page_table, dec_seq)
    o_s = o_hm.reshape(n_s, n_heads, dec_seq, head_dim).transpose(0, 2, 1, 3).reshape(rows_s, d)

    x32, x16 = proj_res_ln(jnp.concatenate([o, o_s], axis=0), cast_weight(w_o, (0,)), x32,
                           ln_g[1, 1], ln_b[1, 1], alpha=alpha, coef=1.0)
    y_prompt, y_sample = ffn(x32, x16, 1, 1, 2, row_splits=((0, rows_p), (rows_p, rows_s)))

    return (y_prompt.reshape(n_p, seq, d), y_sample.reshape(n_s, dec_seq, d),
            conv_prompt[None], conv_sample[None],
            k_prompt.reshape(n_p, seq, n_heads, head_dim), v_prompt.reshape(n_p, seq, n_heads, head_dim),
            k_sample.reshape(n_s, dec_seq, n_heads, head_dim),
            v_sample.reshape(n_s, dec_seq, n_heads, head_dim))
```

```python
import functools
import math

import jax
import jax.numpy as jnp
from jax import lax
from jax.experimental import pallas as pl
from jax.experimental.pallas import tpu as pltpu

F32 = jnp.float32
BF16 = jnp.bfloat16

LN_EPS = 1e-5
CONV_W = 3
SUBLANES = 8
VMEM_LIMIT_BYTES = 56 * 1024 * 1024
ALL_WEIGHTS_ZERO_LOG2 = -160.0


def _params():
    return pltpu.CompilerParams(vmem_limit_bytes=VMEM_LIMIT_BYTES)


def _pick_tile(total, candidates):
    for c in candidates:
        if total % c == 0:
            return c
    return total


def _cast_once(step, pairs):
    @pl.when(step == 0)
    def _():
        for src, dst in pairs:
            dst[...] = src[...].astype(BF16)


def _cast_kernel(w_ref, o_ref):
    o_ref[...] = w_ref[...].astype(BF16)


def cast_weight(w_stack, lead):
    k_dim, n_dim = w_stack.shape[-2:]
    tk = _pick_tile(k_dim, (512, 256, 128))
    nones = (None,) * len(lead)
    return pl.pallas_call(
        _cast_kernel,
        grid=(k_dim // tk,),
        in_specs=[pl.BlockSpec(nones + (tk, n_dim), lambda k: lead + (k, 0))],
        out_specs=pl.BlockSpec((tk, n_dim), lambda k: (k, 0)),
        out_shape=jax.ShapeDtypeStruct((k_dim, n_dim), BF16),
        compiler_params=_params(),
        name="cast_weight",
    )(w_stack)


def _stack_rows_kernel(top_ref, bottom_ref, x32_ref, x16_ref, *, n_top_tiles):
    x = jnp.where(pl.program_id(0) < n_top_tiles, top_ref[...], bottom_ref[...])
    x32_ref[...] = x
    x16_ref[...] = x.astype(BF16)


def stack_rows(top, bottom):
    (rows_top, d), tm = top.shape, bottom.shape[0]
    assert rows_top % tm == 0 and tm % (2 * SUBLANES) == 0
    n_top_tiles = rows_top // tm
    out_spec = pl.BlockSpec((tm, d), lambda m: (m, 0))
    return pl.pallas_call(
        functools.partial(_stack_rows_kernel, n_top_tiles=n_top_tiles),
        grid=(n_top_tiles + 1,),
        in_specs=[pl.BlockSpec((tm, d), lambda m: (jnp.minimum(m, n_top_tiles - 1), 0)),
                  pl.BlockSpec((tm, d), lambda m: (0, 0))],
        out_specs=[out_spec, out_spec],
        out_shape=[jax.ShapeDtypeStruct((rows_top + tm, d), F32),
                   jax.ShapeDtypeStruct((rows_top + tm, d), BF16)],
        compiler_params=_params(),
        name="stack_rows",
    )(top, bottom)


def _swiglu_in_kernel(x_ref, wa_ref, wg_ref, o_ref, wa16_ref, wg16_ref):
    _cast_once(pl.program_id(1), ((wa_ref, wa16_ref), (wg_ref, wg16_ref)))
    x = x_ref[...]
    a = jnp.dot(x, wa16_ref[...], preferred_element_type=F32)
    g = jnp.dot(x, wg16_ref[...], preferred_element_type=F32)
    o_ref[...] = (a * jax.nn.sigmoid(a) * g).astype(o_ref.dtype)


def swiglu_in(x16, w_ffn_in, layer, half):
    m_rows, d = x16.shape
    f = w_ffn_in.shape[-1] // 2
    tm = _pick_tile(m_rows, (1056, 1024, 512, 256, 128))
    tn = _pick_tile(f, (512, 256, 128))
    n_blocks = f // tn
    return pl.pallas_call(
        _swiglu_in_kernel,
        grid=(n_blocks, m_rows // tm),
        in_specs=[
            pl.BlockSpec((tm, d), lambda n, m: (m, 0)),
            pl.BlockSpec((None, None, d, tn), lambda n, m: (layer, half, 0, n)),
            pl.BlockSpec((None, None, d, tn), lambda n, m: (layer, half, 0, n + n_blocks)),
        ],
        out_specs=pl.BlockSpec((tm, tn), lambda n, m: (m, n)),
        out_shape=jax.ShapeDtypeStruct((m_rows, f), BF16),
        scratch_shapes=[pltpu.VMEM((d, tn), BF16)] * 2,
        compiler_params=_params(),
        name="swiglu_in",
    )(x16, w_ffn_in, w_ffn_in)


def _proj_res_ln_kernel(a_ref, w_ref, res_ref, gain_ref, bias_ref, *out_refs, alpha, coef, chunk):
    for r in range(a_ref.shape[0] // chunk):
        rows = slice(r * chunk, (r + 1) * chunk)
        y = alpha * res_ref[rows, :] + coef * jnp.dot(a_ref[rows, :], w_ref[...],
                                                      preferred_element_type=F32)
        mu = jnp.mean(y, axis=-1, keepdims=True)
        dev = y - mu
        var = jnp.mean(dev * dev, axis=-1, keepdims=True)
        out = dev * lax.rsqrt(var + LN_EPS) * gain_ref[...] + bias_ref[...]
        for o_ref in out_refs:
            o_ref[rows, :] = out.astype(o_ref.dtype)


def proj_res_ln(a16, w16, res32, gain, bias, *, alpha, coef, row_start=0, n_rows=None,
                out_dtypes=(F32, BF16)):
    k_dim = a16.shape[1]
    d = w16.shape[1]
    n_rows = a16.shape[0] if n_rows is None else n_rows
    tiles = (528, 512, 256, 128) if k_dim <= d else (384, 256, 128)
    tm = _pick_tile(math.gcd(n_rows, row_start) if row_start else n_rows, tiles)
    chunk = _pick_tile(tm, (192, 176, 128))
    m0 = row_start // tm
    kern = functools.partial(_proj_res_ln_kernel, alpha=alpha, coef=coef, chunk=chunk)
    return pl.pallas_call(
        kern,
        grid=(n_rows // tm,),
        in_specs=[
            pl.BlockSpec((tm, k_dim), lambda m: (m + m0, 0)),
            pl.BlockSpec((k_dim, d), lambda m: (0, 0), pipeline_mode=pl.Buffered(1)),
            pl.BlockSpec((tm, d), lambda m: (m + m0, 0)),
            pl.BlockSpec((1, d), lambda m: (0, 0)),
            pl.BlockSpec((1, d), lambda m: (0, 0)),
        ],
        out_specs=[pl.BlockSpec((tm, d), lambda m: (m, 0)) for _ in out_dtypes],
        out_shape=[jax.ShapeDtypeStruct((n_rows, d), dt) for dt in out_dtypes],
        compiler_params=_params(),
        name="proj_res_ln",
    )(a16, w16, res32, gain.reshape(1, d), bias.reshape(1, d))


def _conv_taps(wconv_ref):
    wc = wconv_ref[...]
    return wc[0:1], wc[1:2], wc[2:3]


def _gates(x_ref, w16_refs):
    x = x_ref[...]
    b_gate, c_gate, h = (jnp.dot(x, w[...], preferred_element_type=F32) for w in w16_refs)
    return b_gate, c_gate * h


def _mixer_in_long_kernel(x_ref, wb_ref, wc_ref, wh_ref, wconv_ref, g_ref, st_ref,
                          wb16_ref, wc16_ref, wh16_ref, hist_ref, *, tiles_per_seq, tm):
    m = pl.program_id(1)
    w16 = (wb16_ref, wc16_ref, wh16_ref)
    _cast_once(m, zip((wb_ref, wc_ref, wh_ref), w16))
    b_gate, u = _gates(x_ref, w16)

    @pl.when(m % tiles_per_seq == 0)
    def _():
        hist_ref[0:SUBLANES, :] = jnp.zeros((SUBLANES, hist_ref.shape[1]), F32)

    hist_ref[SUBLANES:SUBLANES + tm, :] = u
    w0, w1, w2 = _conv_taps(wconv_ref)
    conv = (w0 * hist_ref[SUBLANES - 2:SUBLANES - 2 + tm, :]
            + w1 * hist_ref[SUBLANES - 1:SUBLANES - 1 + tm, :]
            + w2 * u)
    g_ref[...] = (b_gate * conv).astype(g_ref.dtype)
    st_ref[0] = u[tm - (CONV_W - 1):tm, :]
    hist_ref[0:SUBLANES, :] = u[tm - SUBLANES:tm, :]


def _mixer_w_specs(d, tn, n_blocks, index):
    return [pl.BlockSpec((None, d, tn), index(j * n_blocks)) for j in range(3)]


def mixer_in_long(x16, w_a_in, w_a_conv, layer, n_seq, seq_len):
    d = x16.shape[1]
    tm = _pick_tile(seq_len, (512, 256, 128))
    tn = _pick_tile(d, (512, 256, 128))
    n_blocks = d // tn
    tiles_per_seq = seq_len // tm
    kern = functools.partial(_mixer_in_long_kernel, tiles_per_seq=tiles_per_seq, tm=tm)
    return pl.pallas_call(
        kern,
        grid=(n_blocks, n_seq * tiles_per_seq),
        in_specs=[pl.BlockSpec((tm, d), lambda n, m: (m, 0))]
        + _mixer_w_specs(d, tn, n_blocks, lambda off: (lambda n, m: (layer, 0, n + off)))
        + [pl.BlockSpec((None, CONV_W, tn), lambda n, m: (layer, 0, n))],
        out_specs=[
            pl.BlockSpec((tm, tn), lambda n, m: (m, n)),
            pl.BlockSpec((1, CONV_W - 1, tn), lambda n, m: (m // tiles_per_seq, 0, n)),
        ],
        out_shape=[jax.ShapeDtypeStruct((n_seq * seq_len, d), BF16),
                   jax.ShapeDtypeStruct((n_seq, CONV_W - 1, d), F32)],
        scratch_shapes=[pltpu.VMEM((d, tn), BF16)] * 3 + [pltpu.VMEM((tm + SUBLANES, tn), F32)],
        compiler_params=_params(),
        name="mixer_in_long",
    )(x16, w_a_in, w_a_in, w_a_in, w_a_conv)


def _mixer_in_short_kernel(x_ref, wb_ref, wc_ref, wh_ref, wconv_ref, prev_ref, g_ref, st_ref,
                           *, seq_len):
    rows, tn = x_ref.shape[0], wb_ref.shape[1]
    n_seq = rows // seq_len
    x = x_ref[...]
    b_gate, c_gate, h = (jnp.dot(x, w[...].astype(BF16), preferred_element_type=F32)
                         for w in (wb_ref, wc_ref, wh_ref))
    u3 = (c_gate * h).reshape(n_seq, seq_len, tn)
    prev = prev_ref[...]
    p0, p1 = prev[:, 0:1, :], prev[:, 1:2, :]
    pos = lax.broadcasted_iota(jnp.int32, u3.shape, 1)
    u1 = jnp.where(pos == 0, p1, pltpu.roll(u3, 1, axis=1))
    u2 = jnp.where(pos == 0, p0, jnp.where(pos == 1, p1, pltpu.roll(u3, 2, axis=1)))
    w0, w1, w2 = _conv_taps(wconv_ref)
    conv = w0 * u2 + w1 * u1 + w2 * u3
    g_ref[...] = (b_gate * conv.reshape(rows, tn)).astype(g_ref.dtype)
    st_ref[...] = u3[:, seq_len - (CONV_W - 1):seq_len, :]


def mixer_in_short(x16, w_a_in, w_a_conv, prev, layer, row_start, n_seq, seq_len):
    d = x16.shape[1]
    rows = n_seq * seq_len
    assert seq_len == SUBLANES and row_start % rows == 0
    tn = _pick_tile(d, (512, 256, 128))
    n_blocks = d // tn
    kern = functools.partial(_mixer_in_short_kernel, seq_len=seq_len)
    return pl.pallas_call(
        kern,
        grid=(n_blocks,),
        in_specs=[pl.BlockSpec((rows, d), lambda n: (row_start // rows, 0))]
        + _mixer_w_specs(d, tn, n_blocks, lambda off: (lambda n: (layer, 0, n + off)))
        + [pl.BlockSpec((None, CONV_W, tn), lambda n: (layer, 0, n)),
           pl.BlockSpec((None, n_seq, CONV_W - 1, tn), lambda n: (layer, 0, 0, n))],
        out_specs=[
            pl.BlockSpec((rows, tn), lambda n: (0, n)),
            pl.BlockSpec((n_seq, CONV_W - 1, tn), lambda n: (0, 0, n)),
        ],
        out_shape=[jax.ShapeDtypeStruct((rows, d), BF16),
                   jax.ShapeDtypeStruct((n_seq, CONV_W - 1, d), F32)],
        compiler_params=_params(),
        name="mixer_in_short",
    )(x16, w_a_in, w_a_in, w_a_in, w_a_conv, prev)


def _kv_proj_kernel(x_ref, wk_ref, wv_ref, k32_ref, v32_ref, k16_ref, v16_ref, *, head_dim):
    x = x_ref[...]
    tm, heads = k32_ref.shape[:2]
    for w_ref, o32_ref, o16_ref in ((wk_ref, k32_ref, k16_ref), (wv_ref, v32_ref, v16_ref)):
        y = jnp.dot(x, w_ref[...], preferred_element_type=F32)
        o16_ref[...] = y.astype(BF16)
        rows = o32_ref.reshape(tm * heads, head_dim)
        for h in range(heads):
            rows[pl.ds(h, tm, stride=heads), :] = y[:, h * head_dim:(h + 1) * head_dim]


def kv_proj(x16, w_kv16, row_start, n_rows, n_heads, head_dim):
    d = x16.shape[1]
    tm = _pick_tile(math.gcd(n_rows, row_start) if row_start else n_rows, (512, 256, 128))
    heads_per_block = SUBLANES
    tn = heads_per_block * head_dim
    n_blocks = d // tn
    m0 = row_start // tm
    spec32 = pl.BlockSpec((tm, heads_per_block, head_dim), lambda n, m: (m, n, 0))
    spec16 = pl.BlockSpec((tm, tn), lambda n, m: (m, n))
    return pl.pallas_call(
        functools.partial(_kv_proj_kernel, head_dim=head_dim),
        grid=(n_blocks, n_rows // tm),
        in_specs=[
            pl.BlockSpec((tm, d), lambda n, m: (m + m0, 0)),
            pl.BlockSpec((d, tn), lambda n, m: (0, n)),
            pl.BlockSpec((d, tn), lambda n, m: (0, n + n_blocks)),
        ],
        out_specs=[spec32, spec32, spec16, spec16],
        out_shape=[jax.ShapeDtypeStruct((n_rows, n_heads, head_dim), F32)] * 2
                  + [jax.ShapeDtypeStruct((n_rows, d), BF16)] * 2,
        compiler_params=_params(),
        name="kv_proj",
    )(x16, w_kv16, w_kv16)


def _q_proj_kernel(x_ref, w_ref, o_ref, w16_ref, *, scale):
    _cast_once(pl.program_id(1), ((w_ref, w16_ref),))
    q = jnp.dot(x_ref[...], w16_ref[...], preferred_element_type=F32)
    o_ref[...] = (q * scale).astype(o_ref.dtype)


def q_proj(x16, w_q, layer, scale):
    m_rows, d = x16.shape
    tm = _pick_tile(m_rows, (1056, 1024, 512, 256, 128))
    tn = _pick_tile(d, (512, 256, 128))
    return pl.pallas_call(
        functools.partial(_q_proj_kernel, scale=scale),
        grid=(d // tn, m_rows // tm),
        in_specs=[
            pl.BlockSpec((tm, d), lambda n, m: (m, 0)),
            pl.BlockSpec((None, d, tn), lambda n, m: (layer, 0, n)),
        ],
        out_specs=pl.BlockSpec((tm, tn), lambda n, m: (m, n)),
        out_shape=jax.ShapeDtypeStruct((m_rows, d), BF16),
        scratch_shapes=[pltpu.VMEM((d, tn), BF16)],
        compiler_params=_params(),
        name="q_proj",
    )(x16, w_q)


def _suffix_matrix(n):
    j = lax.broadcasted_iota(jnp.int32, (2 * n, n), 0) % n
    s = lax.broadcasted_iota(jnp.int32, (2 * n, n), 1)
    return (j > s).astype(BF16)


def _log2_keep(zn, causal):
    lk = jnp.minimum(zn, 0.0) - jnp.log2(1.0 + jnp.exp2(-jnp.abs(zn)))
    return lk if causal is None else jnp.where(causal, lk, 0.0)


def _stick_weights(zn, lk, suffix, carry):
    hi = lk.astype(BF16)
    lo = (lk - hi.astype(F32)).astype(BF16)
    after = jnp.dot(jnp.concatenate([hi, lo], axis=1), suffix, preferred_element_type=F32)
    w = jnp.exp2((lk - zn) + after + carry)
    return w, carry + jnp.sum(lk, axis=-1, keepdims=True)


_NT = (((1,), (1,)), ((), ()))


def _prompt_tile(qi, q_ref, k_ref, v_ref, suffix_ref, o_ref, *, tq, sub):
    q = q_ref[...]
    suffix = suffix_ref[...]
    head_dim = q.shape[1]
    n_sub = tq // sub

    def visit(key_start, n_blocks, carry, acc, causal):
        k = k_ref[pl.ds(key_start, n_blocks * sub), :]
        v = v_ref[pl.ds(key_start, n_blocks * sub), :]
        zn = lax.dot_general(q, k, _NT, preferred_element_type=F32)
        lk = _log2_keep(zn, causal)
        ws = [None] * n_blocks
        for s in reversed(range(n_blocks)):
            cols = slice(s * sub, (s + 1) * sub)
            ws[s], carry = _stick_weights(zn[:, cols], lk[:, cols], suffix, carry)
        w = jnp.concatenate(ws, axis=1) if n_blocks > 1 else ws[0]
        if causal is not None:
            w = jnp.where(causal, w, 0.0)
        return carry, acc + jnp.dot(w.astype(BF16), v, preferred_element_type=F32)

    own = (lax.broadcasted_iota(jnp.int32, (tq, tq), 1) < lax.broadcasted_iota(jnp.int32, (tq, tq), 0))
    carry, acc = visit(pl.multiple_of(qi * tq, tq), n_sub, jnp.zeros((tq, 1), F32),
                       jnp.zeros((tq, head_dim), F32), own)

    n_older = qi * n_sub

    def any_alive(carry):
        return jnp.max(carry) > ALL_WEIGHTS_ZERO_LOG2

    def cond(loop):
        i, alive, _, _ = loop
        return jnp.logical_and(i < n_older, alive)

    def body(loop):
        i, _, carry, acc = loop
        carry, acc = visit(pl.multiple_of((n_older - 1 - i) * sub, sub), 1, carry, acc, None)
        return i + 1, any_alive(carry), carry, acc

    _, _, _, acc = lax.while_loop(cond, body, (0, any_alive(carry), carry, acc))
    o_ref[...] = acc.astype(o_ref.dtype)


def _sample_step(t, n_steps, last, pt_ref, q_ref, knew_ref, vnew_ref, ck_ref, cv_ref, suffix_ref, o_ref,
                 kbuf, vbuf, sem, acc_ref, carry_ref, kt_ref, *, pages_per_step, n_heads, dec_seq):
    c = t % n_steps
    slot = t % 2
    head_dim = q_ref.shape[2]
    half_heads = n_heads // 2
    page = kbuf.shape[3]
    rows_per_pair = 2 * dec_seq
    suffix = suffix_ref[...]

    def page_copies(step, dst_slot):
        seq, grp = step // n_steps, step % n_steps
        copies = []
        for i in range(pages_per_step):
            pid = pt_ref[seq, (n_steps - 1 - grp) * pages_per_step + (pages_per_step - 1 - i)]
            keys = pl.ds(pl.multiple_of(pid * page, page), page)
            for half in range(2):
                heads = pl.ds(half * half_heads, half_heads)
                for src, dst in ((ck_ref, kbuf), (cv_ref, vbuf)):
                    copies.append(pltpu.make_async_copy(
                        src.at[keys, heads, :], dst.at[dst_slot, i, half], sem.at[dst_slot]))
        return copies

    def start_all(copies):
        for n, cp in enumerate(copies):
            cp.start(priority=n % 2)

    @pl.when(t == 0)
    def _():
        start_all(page_copies(t, slot))

    @pl.when(t < last)
    def _():
        start_all(page_copies(t + 1, 1 - slot))

    def visit(k_head, v_head, causal):
        zs = []
        for hp in range(n_heads // 2):
            lhs = q_ref[0, hp * rows_per_pair:(hp + 1) * rows_per_pair, :]
            for half in range(2):
                h = 2 * hp + half
                kt_ref[h] = k_head(h).T
                z_h = jnp.dot(lhs, kt_ref[h].astype(BF16), preferred_element_type=F32)
                zs.append(z_h[half * dec_seq:(half + 1) * dec_seq])
        zn = jnp.concatenate(zs, axis=0)
        w, carry = _stick_weights(zn, _log2_keep(zn, causal), suffix, carry_ref[...])
        if causal is not None:
            w = jnp.where(causal, w, 0.0)
        carry_ref[...] = carry
        w16 = w.astype(BF16)
        outs = []
        for hp in range(n_heads // 2):
            lhs = w16[hp * rows_per_pair:(hp + 1) * rows_per_pair, :]
            for half in range(2):
                o_h = jnp.dot(lhs, v_head(2 * hp + half).astype(BF16), preferred_element_type=F32)
                outs.append(o_h[half * dec_seq:(half + 1) * dec_seq])
        acc_ref[...] += jnp.concatenate(outs, axis=0)

    def lane_slab(ref):
        return lambda h: ref[0, :, h * head_dim:(h + 1) * head_dim].astype(F32)

    def strided_rows(buf, i):
        def get(h):
            rows = buf.at[slot, i, h // half_heads].reshape(page * half_heads, head_dim)
            return rows[pl.ds(h % half_heads, page, stride=half_heads), :]
        return get

    @pl.when(c == 0)
    def _():
        acc_ref[...] = jnp.zeros_like(acc_ref)
        carry_ref[...] = jnp.zeros_like(carry_ref)
        n_new = knew_ref.shape[1]
        row = lax.broadcasted_iota(jnp.int32, (n_heads * dec_seq, n_new), 0)
        col = lax.broadcasted_iota(jnp.int32, (n_heads * dec_seq, n_new), 1)
        visit(lane_slab(knew_ref), lane_slab(vnew_ref), col < (row % dec_seq))

    for cp in page_copies(t, slot):
        cp.wait()

    for i in range(pages_per_step):
        visit(strided_rows(kbuf, i), strided_rows(vbuf, i), None)

    @pl.when(c == n_steps - 1)
    def _():
        o_ref[0] = acc_ref[...].astype(o_ref.dtype)


def _sb_attention_kernel(pt_ref, pq_ref, pk_ref, pv_ref, psuffix_ref, sq_ref, knew_ref, vnew_ref,
                         ck_ref, cv_ref, ssuffix_ref, po_ref, so_ref, *scratch,
                         n_prompt, n_q, tq, sub, n_sample, n_steps, pages_per_step, n_heads, dec_seq):
    s = pl.program_id(0)

    def sample():
        _sample_step(s, n_steps, n_sample - 1, pt_ref, sq_ref, knew_ref, vnew_ref, ck_ref, cv_ref,
                     ssuffix_ref, so_ref, *scratch, pages_per_step=pages_per_step,
                     n_heads=n_heads, dec_seq=dec_seq)

    def prompt():
        _prompt_tile(s % n_q, pq_ref, pk_ref, pv_ref, psuffix_ref, po_ref, tq=tq, sub=sub)

    for count, part in ((n_sample, sample), (n_prompt, prompt)):
        if count == max(n_sample, n_prompt):
            part()
        else:
            pl.when(s < count)(part)


def sb_attention(q16, k16_p, v16_p, n_seq, seq_len, q_hm, knew_pad, vnew_pad, cache_k, cache_v,
                 page_table, dec_seq):
    n_s, rows, head_dim = q_hm.shape
    n_phys, page, n_heads, _ = cache_k.shape
    n_pages = page_table.shape[1]
    pages_per_step = _pick_tile(n_pages, (8, 4, 2, 1))
    n_steps = n_pages // pages_per_step
    assert n_heads % (2 * SUBLANES) == 0
    half_heads = n_heads // 2
    n_sample = n_s * n_steps

    tq = _pick_tile(seq_len, (512, 256, 128))
    sub = min(256, tq)
    n_q = seq_len // tq
    n_prompt = n_seq * n_heads * n_q

    def prompt_tile(s):
        s = jnp.minimum(s, n_prompt - 1)
        return s // (n_heads * n_q), (s // n_q) % n_heads, s % n_q

    def q_index(s, pt):
        b, h, i = prompt_tile(s)
        return b * n_q + i, h

    def kv_index(s, pt):
        b, h, _ = prompt_tile(s)
        return b, h

    def seq_index(s, pt):
        return jnp.minimum(s, n_sample - 1) // n_steps, 0, 0

    kern = functools.partial(_sb_attention_kernel, n_prompt=n_prompt, n_q=n_q, tq=tq, sub=sub,
                             n_sample=n_sample, n_steps=n_steps, pages_per_step=pages_per_step,
                             n_heads=n_heads, dec_seq=dec_seq)
    page_buffers = pltpu.VMEM((2, pages_per_step, 2, page, half_heads, head_dim), F32)
    grid_spec = pltpu.PrefetchScalarGridSpec(
        num_scalar_prefetch=1,
        grid=(max(n_prompt, n_sample),),
        in_specs=[
            pl.BlockSpec((tq, head_dim), q_index),
            pl.BlockSpec((seq_len, head_dim), kv_index),
            pl.BlockSpec((seq_len, head_dim), kv_index),
            pl.BlockSpec((2 * sub, sub), lambda s, pt: (0, 0)),
            pl.BlockSpec((1, rows, head_dim), seq_index),
            pl.BlockSpec((1, page, n_heads * head_dim), seq_index),
            pl.BlockSpec((1, page, n_heads * head_dim), seq_index),
            pl.BlockSpec(memory_space=pl.ANY),
            pl.BlockSpec(memory_space=pl.ANY),
            pl.BlockSpec((2 * page, page), lambda s, pt: (0, 0)),
        ],
        out_specs=[pl.BlockSpec((tq, head_dim), q_index),
                   pl.BlockSpec((1, rows, head_dim), seq_index)],
        scratch_shapes=[page_buffers, page_buffers, pltpu.SemaphoreType.DMA((2,)),
                        pltpu.VMEM((rows, head_dim), F32), pltpu.VMEM((rows, 1), F32),
                        pltpu.VMEM((n_heads, head_dim, page), F32)],
    )
    ck = cache_k.reshape(n_phys * page, n_heads, head_dim)
    cv = cache_v.reshape(n_phys * page, n_heads, head_dim)
    return pl.pallas_call(
        kern,
        grid_spec=grid_spec,
        out_shape=[jax.ShapeDtypeStruct(k16_p.shape, BF16),
                   jax.ShapeDtypeStruct((n_s, rows, head_dim), BF16)],
        compiler_params=pltpu.CompilerParams(vmem_limit_bytes=VMEM_LIMIT_BYTES,
                                             dimension_semantics=("arbitrary",)),
        name="sb_attention",
    )(page_table, q16, k16_p, v16_p, _suffix_matrix(sub), q_hm, knew_pad, vnew_pad, ck, cv,
      _suffix_matrix(page))


def kernel(x_prompt, x_sample, state_conv, cache_k, cache_v, page_table, ln_g, ln_b, w_ffn_in,
           w_ffn_out, w_a_in, w_a_conv, w_a_out, w_kv, w_q, w_o):
    n_p, seq, d = x_prompt.shape
    n_s, dec_seq, _ = x_sample.shape
    depth = ln_g.shape[0]
    page, n_heads, head_dim = cache_k.shape[1:]
    rows_p, rows_s = n_p * seq, n_s * dec_seq
    alpha = (2.0 * depth) ** 0.25
    assert w_a_in.shape[0] == 1 and depth == 2, "one short-conv layer followed by one attention layer"

    x32, x16 = stack_rows(x_prompt.reshape(rows_p, d), x_sample.reshape(rows_s, d))

    def ffn(x32, x16, layer, half, ln_idx, row_splits=None):
        h = swiglu_in(x16, w_ffn_in, layer, half)
        w_out16 = cast_weight(w_ffn_out, (layer, half))
        norm = functools.partial(proj_res_ln, h, w_out16, x32, ln_g[layer, ln_idx], ln_b[layer, ln_idx],
                                 alpha=alpha, coef=0.5)
        if row_splits is None:
            return norm()
        return [norm(row_start=r0, n_rows=n, out_dtypes=(F32,))[0] for r0, n in row_splits]

    x32, x16 = ffn(x32, x16, 0, 0, 0)
    g, conv_prompt = mixer_in_long(x16, w_a_in, w_a_conv, 0, n_p, seq)
    g_s, conv_sample = mixer_in_short(x16, w_a_in, w_a_conv, state_conv, 0, rows_p, n_s, dec_seq)
    x32, x16 = proj_res_ln(jnp.concatenate([g, g_s], axis=0), cast_weight(w_a_out, (0,)), x32,
                           ln_g[0, 1], ln_b[0, 1], alpha=alpha, coef=1.0)
    x32, x16 = ffn(x32, x16, 0, 1, 2)

    w_kv16 = w_kv.astype(BF16)
    k_prompt, v_prompt, k16_p, v16_p = kv_proj(x16, w_kv16, 0, rows_p, n_heads, head_dim)
    k_sample, v_sample, k16_s, v16_s = kv_proj(x16, w_kv16, rows_p, rows_s, n_heads, head_dim)

    x32, x16 = ffn(x32, x16, 1, 0, 0)
    q16 = q_proj(x16, w_q, 0, -math.log2(math.e) * head_dim ** -0.5)
    q_hm = (q16[rows_p:].reshape(n_s, dec_seq, n_heads, head_dim)
            .transpose(0, 2, 1, 3).reshape(n_s, n_heads * dec_seq, head_dim))
    pad = ((0, 0), (0, page - dec_seq), (0, 0))
    o, o_hm = sb_attention(q16, k16_p, v16_p, n_p, seq, q_hm,
                           jnp.pad(k16_s.reshape(n_s, dec_seq, d), pad),
                           jnp.pad(v16_s.reshape(n_s, dec_seq, d), pad),
                           cache_k, cache_v, page_table, dec_seq)
    o_s = o_hm.reshape(n_s, n_heads, dec_seq, head_dim).transpose(0, 2, 1, 3).reshape(rows_s, d)

    x32, x16 = proj_res_ln(jnp.concatenate([o, o_s], axis=0), cast_weight(w_o, (0,)), x32,
                           ln_g[1, 1], ln_b[1, 1], alpha=alpha, coef=1.0)
    y_prompt, y_sample = ffn(x32, x16, 1, 1, 2, row_splits=((0, rows_p), (rows_p, rows_s)))

    return (y_prompt.reshape(n_p, seq, d), y_sample.reshape(n_s, dec_seq, d),
            conv_prompt[None], conv_sample[None],
            k_prompt.reshape(n_p, seq, n_heads, head_dim), v_prompt.reshape(n_p, seq, n_heads, head_dim),
            k_sample.reshape(n_s, dec_seq, n_heads, head_dim),
            v_sample.reshape(n_s, dec_seq, n_heads, head_dim))
```
